```python
import math
import jax, jax.numpy as jnp
from jax import lax
import numpy as np

D_MODEL = 1024
BATCH = 32
SEQ = 2048
DEPTH = 2

CTX_LEN = 256
GRID_W = 64
N_EVEN = (DEPTH + 1) // 2
N_ODD = DEPTH // 2

A_GROUPS = 4
A_GROUP_DIM = 128
A_WIDTH = A_GROUPS * A_GROUP_DIM
A_CHUNK = 128
B_HEADS = 4
B_HEAD_DIM = 128
B_WIDTH = B_HEADS * B_HEAD_DIM
B_CONV = 3
GDN_CHUNK = 64
AB_SPLITS = (A_WIDTH, 2 * A_WIDTH, 2 * A_WIDTH + 3 * B_WIDTH, 2 * A_WIDTH + 4 * B_WIDTH,
             2 * A_WIDTH + 4 * B_WIDTH + 2 * B_HEADS)
AB_IN = 2 * A_WIDTH + 4 * B_WIDTH + 4 * B_HEADS
AB_MIX = A_WIDTH + B_WIDTH
C_HEADS = 8
C_NOPE = 128
C_ROPE = 64
C_VDIM = 128
C_Q_LORA = 384
C_KV_LORA = 256
C_IN = C_Q_LORA + C_KV_LORA + C_ROPE
Q_BLOCK = 128
ROPE_THETA = 10000.0
D_FF = 2816
FFN_CONV = 3
EPS = 1e-6

kernel_name = 'hybrid_gmlp_gdn_mla_convffn_prefix'


def rms_norm(x, g):
    xf = x.astype(jnp.float32)
    y = xf * lax.rsqrt(jnp.mean(xf * xf, axis=-1, keepdims=True) + EPS)
    return (y * g.astype(jnp.float32)).astype(x.dtype)


def layer_norm(x, g, b):
    xf = x.astype(jnp.float32)
    mu = jnp.mean(xf, axis=-1, keepdims=True)
    var = jnp.mean(jnp.square(xf - mu), axis=-1, keepdims=True)
    y = (xf - mu) * lax.rsqrt(var + EPS) * g.astype(jnp.float32) + b.astype(jnp.float32)
    return y.astype(x.dtype)


def modulate(h, shift, scale):
    return h * (1 + scale) + shift


def ada_mod(cond, w, b):
    m = (jax.nn.silu(cond) @ w + b)[..., None, :]
    return jnp.split(m, 6, axis=-1)


def dwconv(x, w):
    k = w.shape[0]
    return lax.conv_general_dilated(x, w[:, None, :], (1,), [(k // 2, k // 2)],
                                    dimension_numbers=('NWC', 'WIO', 'NWC'),
                                    feature_group_count=x.shape[-1])


def axial_rope(n, rot_dim):
    rows = n // GRID_W
    row = jnp.repeat(jnp.arange(rows, dtype=jnp.float32), GRID_W)
    col = jnp.tile(jnp.arange(GRID_W, dtype=jnp.float32), rows)
    n_freq = rot_dim // 4
    inv = ROPE_THETA ** (-jnp.arange(n_freq, dtype=jnp.float32) / n_freq)
    ang = jnp.concatenate([row[:, None] * inv, col[:, None] * inv], axis=-1)
    return jnp.cos(ang), jnp.sin(ang)


def apply_rope(x, cos, sin):
    h = x.shape[-1] // 2
    x1, x2 = x[..., :h], x[..., h:]
    cos = cos.astype(x.dtype)
    sin = sin.astype(x.dtype)
    return jnp.concatenate([x1 * cos - x2 * sin, x2 * cos + x1 * sin], axis=-1)


def chunk_gmlp(u, v, ln_g, ln_b, ws, bs):
    u = jax.nn.gelu(u, approximate=False)
    v = layer_norm(jax.nn.gelu(v, approximate=False), ln_g, ln_b)
    bn, l, _ = v.shape
    vc = v.reshape(bn, l // A_CHUNK, A_CHUNK, A_GROUPS, A_GROUP_DIM)
    s = jnp.einsum('gij,bnjgd->bnigd', ws, vc) + bs.T[None, None, :, :, None]
    return u * s.reshape(bn, l, A_WIDTH)


def gdn_chunked(q, k, v, g, beta, s0):
    bn, l, h, dk = q.shape
    cs = GDN_CHUNK
    n = l // cs
    to_c = lambda t: t.reshape(bn, n, cs, h, t.shape[-1]).transpose(1, 0, 3, 2, 4)
    qc, kc, vc = to_c(q), to_c(k), to_c(v)
    gc = g.reshape(bn, n, cs, h).transpose(1, 0, 3, 2)
    bc = beta.reshape(bn, n, cs, h).transpose(1, 0, 3, 2)
    gcum = jnp.cumsum(gc, axis=-1)
    tri_incl = jnp.tril(jnp.ones((cs, cs), bool))
    tri_strict = jnp.tril(jnp.ones((cs, cs), bool), -1)
    diff = gcum[..., :, None] - gcum[..., None, :]
    decay = jnp.exp(jnp.where(tri_incl, diff, -jnp.inf))
    kb = kc * bc[..., None]
    vb = vc * bc[..., None]
    a_mat = jnp.where(tri_strict, jnp.einsum('nbhid,nbhjd->nbhij', kb, kc) * decay, 0.0)
    eye = jnp.eye(cs, dtype=jnp.float32)
    t_mat = lax.linalg.triangular_solve(eye + a_mat, jnp.broadcast_to(eye, a_mat.shape),
                                        left_side=True, lower=True, unit_diagonal=True)
    u = jnp.einsum('nbhij,nbhjd->nbhid', t_mat, vb)
    w = jnp.einsum('nbhij,nbhjd->nbhid', t_mat, kb * jnp.exp(gcum)[..., None])
    attn = jnp.einsum('nbhid,nbhjd->nbhij', qc, kc) * decay
    qg = qc * jnp.exp(gcum)[..., None]
    kdec = kc * jnp.exp(gcum[..., -1:] - gcum)[..., None]
    gend = jnp.exp(gcum[..., -1])

    def step(s, xs):
        u_i, w_i, qg_i, kd_i, at_i, ge_i = xs
        v_new = u_i - jnp.einsum('bhck,bhkv->bhcv', w_i, s)
        o_i = jnp.einsum('bhck,bhkv->bhcv', qg_i, s) + jnp.einsum('bhij,bhjv->bhiv', at_i, v_new)
        s = s * ge_i[..., None, None] + jnp.einsum('bhck,bhcv->bhkv', kd_i, v_new)
        return s, o_i

    s_fin, o = lax.scan(step, s0, (u, w, qg, kdec, attn, gend))
    o = o.transpose(1, 0, 3, 2, 4).reshape(bn, l, h, v.shape[-1])
    return o, s_fin


def gated_deltanet(qkv, gate, alpha_raw, beta_raw, conv_w, a_log, dt_bias, norm_g, s0_pair):
    bn, l, _ = qkv.shape
    hq = jax.nn.silu(dwconv(qkv, conv_w)).astype(jnp.float32)
    q, k, v = jnp.split(hq, 3, axis=-1)
    l2 = lambda t: t * lax.rsqrt(jnp.sum(t * t, axis=-1, keepdims=True) + EPS)
    q = l2(q.reshape(bn, l, B_HEADS, B_HEAD_DIM)) * (B_HEAD_DIM ** -0.5)
    k = l2(k.reshape(bn, l, B_HEADS, B_HEAD_DIM))
    v = v.reshape(bn, l, B_HEADS, B_HEAD_DIM)
    a = alpha_raw.astype(jnp.float32).reshape(bn, l, 2, B_HEADS)
    beta = jax.nn.sigmoid(beta_raw.astype(jnp.float32).reshape(bn, l, 2, B_HEADS))
    log_alpha = -jnp.exp(a_log.astype(jnp.float32)) * jax.nn.softplus(a + dt_bias.astype(jnp.float32))
    o_f, s_f = gdn_chunked(q, k, v, log_alpha[:, :, 0], beta[:, :, 0], s0_pair[0])
    flip = lambda t: jnp.flip(t, axis=1)
    o_b, s_b = gdn_chunked(flip(q), flip(k), flip(v), flip(log_alpha[:, :, 1]), flip(beta[:, :, 1]), s0_pair[1])
    o = o_f + flip(o_b)
    o = rms_norm(o, norm_g) * jax.nn.silu(gate.astype(jnp.float32).reshape(bn, l, B_HEADS, B_HEAD_DIM))
    return o.reshape(bn, l, B_WIDTH).astype(qkv.dtype), (s_f, s_b)


def ab_mixer(h_ctx, h_lat, w_in, a_ln_g, a_ln_b, a_ws, a_bs, conv_w, a_log, dt_bias, norm_g, w_out,
             need_ctx_out):
    def run(h, state0, want_out):
        a_u, a_v, qkv, gate, alpha_raw, beta_raw = jnp.split(h @ w_in, AB_SPLITS, axis=-1)
        y_b, state = gated_deltanet(qkv, gate, alpha_raw, beta_raw, conv_w, a_log, dt_bias, norm_g, state0)
        if not want_out:
            return None, state
        y_a = chunk_gmlp(a_u, a_v, a_ln_g, a_ln_b, a_ws, a_bs)
        return jnp.concatenate([y_a, y_b], axis=-1) @ w_out, state

    zero = jnp.zeros((h_ctx.shape[0], B_HEADS, B_HEAD_DIM, B_HEAD_DIM), jnp.float32)
    y_ctx, ctx_state = run(h_ctx, (zero, zero), need_ctx_out)
    y_lat, _ = run(h_lat, ctx_state, True)
    return y_ctx, y_lat


def block_attention(qn, qr, kn, kr, v):
    bn, l, h, _ = qn.shape
    nb = l // Q_BLOCK
    scale = (C_NOPE + C_ROPE) ** -0.5
    blocks = lambda t: t.reshape(bn, nb, Q_BLOCK, h, t.shape[-1]).transpose(1, 0, 2, 3, 4)

    def one(blk):
        qn_b, qr_b = blk
        s = jnp.einsum('bqhd,bkhd->bhqk', qn_b, kn) + jnp.einsum('bqhd,bkd->bhqk', qr_b, kr)
        p = jax.nn.softmax(s.astype(jnp.float32) * scale, axis=-1).astype(v.dtype)
        return jnp.einsum('bhqk,bkhd->bqhd', p, v)

    o = lax.map(one, (blocks(qn), blocks(qr)))
    return o.transpose(1, 0, 2, 3, 4).reshape(bn, l, h * v.shape[-1])


def mla_mixer(h_ctx, h_lat, w_in, q_norm_g, kv_norm_g, w_uq, w_ukv, w_out, need_ctx_out):
    n_lat = h_lat.shape[1]
    cos, sin = axial_rope(n_lat, C_ROPE)
    rot_q = lambda t: apply_rope(t, cos[:, None, :], sin[:, None, :])
    rot_k = lambda t: apply_rope(t, cos, sin)
    no_rot = lambda t: t

    def queries(cq, rot):
        bn, l, _ = cq.shape
        q = (rms_norm(cq, q_norm_g) @ w_uq).reshape(bn, l, C_HEADS, C_NOPE + C_ROPE)
        return q[..., :C_NOPE], rot(q[..., C_NOPE:])

    def keys(kv_part, rot):
        bn, l, _ = kv_part.shape
        ckv, kr = jnp.split(kv_part, [C_KV_LORA], axis=-1)
        kv = (rms_norm(ckv, kv_norm_g) @ w_ukv).reshape(bn, l, C_HEADS, C_NOPE + C_VDIM)
        return kv[..., :C_NOPE], rot(kr), kv[..., C_NOPE:]

    z_lat = h_lat @ w_in
    kn_l, kr_l, v_l = keys(z_lat[..., C_Q_LORA:], rot_k)
    qn_l, qr_l = queries(z_lat[..., :C_Q_LORA], rot_q)
    if need_ctx_out:
        z_ctx = h_ctx @ w_in
        kv_ctx = z_ctx[..., C_Q_LORA:]
    else:
        kv_ctx = h_ctx @ w_in[:, C_Q_LORA:]
    kn_c, kr_c, v_c = keys(kv_ctx, no_rot)
    kn = jnp.concatenate([kn_c, kn_l], axis=1)
    kr = jnp.concatenate([kr_c, kr_l], axis=1)
    v = jnp.concatenate([v_c, v_l], axis=1)
    y_lat = block_attention(qn_l, qr_l, kn, kr, v) @ w_out
    y_ctx = None
    if need_ctx_out:
        qn_c, qr_c = queries(z_ctx[..., :C_Q_LORA], no_rot)
        y_ctx = block_attention(qn_c, qr_c, kn_c, kr_c, v_c) @ w_out
    return y_ctx, y_lat


def conv_ffn(h, w_up, conv_w, w_down):
    g, u = jnp.split(dwconv(h @ w_up, conv_w), 2, axis=-1)
    return (jax.nn.silu(g) * u) @ w_down


def setup_inputs(seed: int = 0) -> dict:
    key = jax.random.key(seed)
    ks = iter(jax.random.split(key, 32))
    nrm = lambda shape, s: jax.random.normal(next(ks), shape, jnp.float32) * s
    gain = lambda shape: 1.0 + nrm(shape, 0.02)
    d = D_MODEL
    a_log = jnp.log(jax.random.uniform(next(ks), (N_EVEN, 2, B_HEADS), jnp.float32, 1.0, 16.0))
    dt = jnp.exp(jax.random.uniform(next(ks), (N_EVEN, 2, B_HEADS), jnp.float32,
                                    math.log(1e-3), math.log(1e-1)))
    dt_bias = dt + jnp.log(-jnp.expm1(-dt))
    return {
        'x': nrm((BATCH, SEQ, d), 1.0),
        'c': nrm((BATCH, d), 1.0),
        'ctx': nrm((BATCH, CTX_LEN, d), 1.0),
        'c_ctx': nrm((d,), 1.0),
        'ada_w': nrm((DEPTH, d, 6 * d), d ** -0.5),
        'ada_b': nrm((DEPTH, 6 * d), 0.01),
        'norm1_g': gain((DEPTH, d)),
        'norm2_g': gain((DEPTH, d)),
        'ab_w_in': nrm((N_EVEN, d, AB_IN), d ** -0.5),
        'a_ln_g': gain((N_EVEN, A_WIDTH)),
        'a_ln_b': nrm((N_EVEN, A_WIDTH), 0.01),
        'a_ws': nrm((N_EVEN, A_GROUPS, A_CHUNK, A_CHUNK), 0.5 * A_CHUNK ** -0.5),
        'a_bs': 1.0 + nrm((N_EVEN, A_GROUPS, A_CHUNK), 0.1),
        'b_conv_w': nrm((N_EVEN, B_CONV, 3 * B_WIDTH), B_CONV ** -0.5),
        'b_a_log': a_log,
        'b_dt_bias': dt_bias,
        'b_norm_g': gain((N_EVEN, B_HEAD_DIM)),
        'ab_w_out': nrm((N_EVEN, AB_MIX, d), AB_MIX ** -0.5),
        'mla_w_in': nrm((N_ODD, d, C_IN), d ** -0.5),
        'mla_q_norm_g': gain((N_ODD, C_Q_LORA)),
        'mla_kv_norm_g': gain((N_ODD, C_KV_LORA)),
        'mla_w_uq': nrm((N_ODD, C_Q_LORA, C_HEADS * (C_NOPE + C_ROPE)), C_Q_LORA ** -0.5),
        'mla_w_ukv': nrm((N_ODD, C_KV_LORA, C_HEADS * (C_NOPE + C_VDIM)), C_KV_LORA ** -0.5),
        'mla_w_out': nrm((N_ODD, C_HEADS * C_VDIM, d), (C_HEADS * C_VDIM) ** -0.5),
        'ffn_w_up': nrm((DEPTH, d, 2 * D_FF), d ** -0.5),
        'ffn_conv_w': nrm((DEPTH, FFN_CONV, 2 * D_FF), FFN_CONV ** -0.5),
        'ffn_w_down': nrm((DEPTH, D_FF, d), D_FF ** -0.5),
        'final_g': gain((d,)),
    }


def reference(x, c, ctx, c_ctx, ada_w, ada_b, norm1_g, norm2_g, ab_w_in, a_ln_g, a_ln_b, a_ws, a_bs,
              b_conv_w, b_a_log, b_dt_bias, b_norm_g, ab_w_out, mla_w_in, mla_q_norm_g, mla_kv_norm_g,
              mla_w_uq, mla_w_ukv, mla_w_out, ffn_w_up, ffn_conv_w, ffn_w_down, final_g):
    lat, cx = x, ctx
    for i in range(DEPTH):
        last = i == DEPTH - 1
        j = i // 2
        sh1, sc1, g1, sh2, sc2, g2 = ada_mod(c, ada_w[i], ada_b[i])
        csh1, csc1, cg1, csh2, csc2, cg2 = ada_mod(c_ctx, ada_w[i], ada_b[i])
        h_lat = modulate(rms_norm(lat, norm1_g[i]), sh1, sc1)
        h_ctx = modulate(rms_norm(cx, norm1_g[i]), csh1, csc1)
        if i % 2 == 0:
            y_ctx, y_lat = ab_mixer(h_ctx, h_lat, ab_w_in[j], a_ln_g[j], a_ln_b[j], a_ws[j], a_bs[j],
                                    b_conv_w[j], b_a_log[j], b_dt_bias[j], b_norm_g[j], ab_w_out[j],
                                    not last)
        else:
            y_ctx, y_lat = mla_mixer(h_ctx, h_lat, mla_w_in[j], mla_q_norm_g[j], mla_kv_norm_g[j],
                                     mla_w_uq[j], mla_w_ukv[j], mla_w_out[j], not last)
        lat = lat + g1 * y_lat
        lat = lat + g2 * conv_ffn(modulate(rms_norm(lat, norm2_g[i]), sh2, sc2),
                                  ffn_w_up[i], ffn_conv_w[i], ffn_w_down[i])
        if not last:
            cx = cx + cg1 * y_ctx
            cx = cx + cg2 * conv_ffn(modulate(rms_norm(cx, norm2_g[i]), csh2, csc2),
                                     ffn_w_up[i], ffn_conv_w[i], ffn_w_down[i])
    return rms_norm(lat, final_g)
```

```python
import functools

import jax
import jax.numpy as jnp
from jax import lax
from jax.experimental import pallas as pl
from jax.experimental.pallas import tpu as pltpu

f32 = jnp.float32
bf16 = jnp.bfloat16

D_MODEL = 1024
GRID_W = 64
A_GROUPS = 4
A_GROUP_DIM = 128
A_WIDTH = A_GROUPS * A_GROUP_DIM
A_CHUNK = 128
B_HEADS = 4
B_HEAD_DIM = 128
B_WIDTH = B_HEADS * B_HEAD_DIM
C_HEADS = 8
C_NOPE = 128
C_ROPE = 64
C_VDIM = 128
C_Q_LORA = 384
C_KV_LORA = 256
ROPE_THETA = 10000.0
D_FF = 2816
EPS = 1e-6

HALO = 8
GDN_CHUNK = 128
FFN_CW = 256
VMEM_LIMIT = 56 * 1024 * 1024

NT = (((1,), (1,)), ((), ()))
TN = (((0,), (0,)), ((), ()))


def _dot(a, b):
    return jnp.dot(a, b, preferred_element_type=f32)


def _silu(x):
    return x * jax.nn.sigmoid(x)


def _gelu(x):
    return 0.5 * x * (1.0 + lax.erf(x * (0.5 ** 0.5)))


def _softplus(x):
    return jnp.maximum(x, 0.0) + jnp.log1p(jnp.exp(-jnp.abs(x)))


def _rms(x, g):
    return x * lax.rsqrt(jnp.mean(x * x, axis=-1, keepdims=True) + EPS) * g


def _norm_mod(x, g, shift, scale):
    return _rms(x, g) * (1.0 + scale) + shift


def _split3(x):
    hi = x.astype(bf16)
    r = x - hi.astype(f32)
    mid = r.astype(bf16)
    lo = (r - mid.astype(f32)).astype(bf16)
    return hi, mid, lo


def _params(n_axes):
    return pltpu.CompilerParams(dimension_semantics=("arbitrary",) * n_axes,
                                vmem_limit_bytes=VMEM_LIMIT)


def _full(shape):
    nd = len(shape)
    return pl.BlockSpec(shape, lambda *_: (0,) * nd, pipeline_mode=pl.Buffered(1))


def _mod_spec(chunk, is_ctx, n_batch):
    if is_ctx:
        return pl.BlockSpec((None, None, 1, D_MODEL), lambda b, i: (n_batch, chunk, 0, 0))
    return pl.BlockSpec((None, None, 1, D_MODEL), lambda b, i: (b, chunk, 0, 0))


def _tok_spec(tm, width):
    return pl.BlockSpec((None, tm, width), lambda b, i: (b, i, 0))


def _halo_specs(tm, t_len):
    r = tm // HALO
    last = t_len // HALO - 1
    return [
        pl.BlockSpec((None, tm, D_MODEL), lambda b, i: (b, i, 0)),
        pl.BlockSpec((None, HALO, D_MODEL), lambda b, i: (b, jnp.maximum(i * r - 1, 0), 0)),
        pl.BlockSpec((None, HALO, D_MODEL), lambda b, i: (b, jnp.minimum((i + 1) * r, last), 0)),
    ]


def _fill_halo_h(x_ref, xp_ref, xn_ref, g_ref, sh_ref, sc_ref, hs_ref, tm):
    i = pl.program_id(1)
    nt = pl.num_programs(1)
    g, sh, sc = g_ref[...], sh_ref[...], sc_ref[...]
    hs_ref[pl.ds(HALO, tm), :] = _norm_mod(x_ref[...], g, sh, sc)
    hs_ref[pl.ds(0, HALO), :] = jnp.where(i > 0, _norm_mod(xp_ref[...], g, sh, sc), 0.0)
    hs_ref[pl.ds(HALO + tm, HALO), :] = jnp.where(i < nt - 1, _norm_mod(xn_ref[...], g, sh, sc), 0.0)


def _conv3(z_ref, cw, tm):
    return (cw[0:1] * z_ref[pl.ds(HALO - 1, tm), :] + cw[1:2] * z_ref[pl.ds(HALO, tm), :]
            + cw[2:3] * z_ref[pl.ds(HALO + 1, tm), :])


def _ada_kernel(c_ref, w_ref, b_ref, o_ref):
    s = _silu(c_ref[...]).astype(bf16)
    o_ref[...] = _dot(s, w_ref[...].astype(bf16)) + b_ref[...]


def _ada_mods(cc, ada_w, ada_b):
    depth = ada_w.shape[0]
    rows = cc.shape[0]
    return pl.pallas_call(
        _ada_kernel,
        out_shape=jax.ShapeDtypeStruct((depth, rows, 6 * D_MODEL), f32),
        grid=(depth, 6),
        in_specs=[pl.BlockSpec((rows, D_MODEL), lambda l, j: (0, 0)),
                  pl.BlockSpec((None, D_MODEL, D_MODEL), lambda l, j: (l, 0, j)),
                  pl.BlockSpec((None, 1, D_MODEL), lambda l, j: (l, 0, j))],
        out_specs=pl.BlockSpec((None, rows, D_MODEL), lambda l, j: (l, 0, j)),
        compiler_params=_params(2),
        name="ada_mods",
    )(cc, ada_w, ada_b.reshape(depth, 1, 6 * D_MODEL))


def _inproj0_kernel(x_ref, xp_ref, xn_ref, g_ref, sh_ref, sc_ref, wqkv_ref, wrest_ref, wab_ref,
                    wabt_ref, conv_ref, auav_ref, gate_ref, q_ref, k_ref, v_ref, ab_ref, abt_ref,
                    hs_ref, zs_ref, *, tm):
    _fill_halo_h(x_ref, xp_ref, xn_ref, g_ref, sh_ref, sc_ref, hs_ref, tm)
    hb_all = hs_ref[...].astype(bf16)
    hb = hs_ref[pl.ds(HALO, tm), :].astype(bf16)
    auav_ref[:, 0:A_WIDTH] = _dot(hb, wrest_ref[:, 0:A_WIDTH]).astype(bf16)
    auav_ref[:, A_WIDTH:2 * A_WIDTH] = _dot(hb, wrest_ref[:, A_WIDTH:2 * A_WIDTH]).astype(bf16)
    gate_ref[...] = _dot(hb, wrest_ref[:, 2 * A_WIDTH:]).astype(bf16)
    ab_ref[...] = _dot(hb, wab_ref[...])[:, 0:4 * B_HEADS]
    abt_ref[...] = lax.dot_general(wabt_ref[...], hb, NT, preferred_element_type=f32)
    for j, o_ref in enumerate((q_ref, k_ref, v_ref)):
        cols = slice(j * B_WIDTH, (j + 1) * B_WIDTH)
        zs_ref[...] = _dot(hb_all, wqkv_ref[:, cols])
        y = _silu(_conv3(zs_ref, conv_ref[:, cols], tm))
        if j == 2:
            o_ref[...] = y.astype(bf16)
            continue
        post = B_HEAD_DIM ** -0.5 if j == 0 else 1.0
        for h in range(B_HEADS):
            yh = y[:, h * B_HEAD_DIM:(h + 1) * B_HEAD_DIM]
            inv = lax.rsqrt(jnp.sum(yh * yh, axis=-1, keepdims=True) + EPS) * post
            o_ref[:, h * B_HEAD_DIM:(h + 1) * B_HEAD_DIM] = (yh * inv).astype(bf16)


def _inproj0(x, mods, norm_g, w, is_ctx, tm):
    n_batch, t_len, _ = x.shape
    nb = mods.shape[0] - 1
    tok = lambda width, dt: jax.ShapeDtypeStruct((n_batch, t_len, width), dt)
    return pl.pallas_call(
        functools.partial(_inproj0_kernel, tm=tm),
        out_shape=(tok(2 * A_WIDTH, bf16), tok(B_WIDTH, bf16), tok(B_WIDTH, bf16), tok(B_WIDTH, bf16),
                   tok(B_WIDTH, bf16), tok(4 * B_HEADS, f32),
                   jax.ShapeDtypeStruct((n_batch, 4 * B_HEADS, t_len), f32)),
        grid=(n_batch, t_len // tm),
        in_specs=_halo_specs(tm, t_len) + [
            _full((1, D_MODEL)), _mod_spec(0, is_ctx, nb), _mod_spec(1, is_ctx, nb),
            _full(w["wqkv"].shape), _full(w["wrest"].shape), _full(w["wab"].shape),
            _full(w["wabt"].shape), _full(w["conv"].shape)],
        out_specs=(_tok_spec(tm, 2 * A_WIDTH), _tok_spec(tm, B_WIDTH), _tok_spec(tm, B_WIDTH),
                   _tok_spec(tm, B_WIDTH), _tok_spec(tm, B_WIDTH), _tok_spec(tm, 4 * B_HEADS),
                   pl.BlockSpec((None, 4 * B_HEADS, tm), lambda b, i: (b, 0, i))),
        scratch_shapes=[pltpu.VMEM((tm + 2 * HALO, D_MODEL), f32),
                        pltpu.VMEM((tm + 2 * HALO, B_WIDTH), f32)],
        compiler_params=_params(2),
        name="inproj0_ctx" if is_ctx else "inproj0_lat",
    )(x, x, x, norm_g, mods, mods, w["wqkv"], w["wrest"], w["wab"], w["wabt"], w["conv"])


def _gdn_kernel(qc_ref, kc_ref, vc_ref, gc_ref, abc_ref, abtc_ref,
                ql_ref, kl_ref, vl_ref, gl_ref, abl_ref, abtl_ref,
                alc_ref, dtc_ref, alr_ref, dtr_ref, ng_ref,
                yc_ref, yl_ref,
                q_s, k_s, v_s, o_s, st_s, gcol_s, bcol_s, gtcol_s, grow_s, gtrow_s,
                *, t_ctx, t_lat):
    C = GDN_CHUNK
    n_ctx, n_lat = t_ctx // C, t_lat // C
    n_chunks = n_ctx + n_lat
    nh = B_HEADS

    q_s[pl.ds(0, t_ctx), :] = qc_ref[...]
    q_s[pl.ds(t_ctx, t_lat), :] = ql_ref[...]
    k_s[pl.ds(0, t_ctx), :] = kc_ref[...]
    k_s[pl.ds(t_ctx, t_lat), :] = kl_ref[...]
    v_s[pl.ds(0, t_ctx), :] = vc_ref[...]
    v_s[pl.ds(t_ctx, t_lat), :] = vl_ref[...]
    o_s[...] = jnp.zeros_like(o_s)
    st_s[...] = jnp.zeros_like(st_s)

    ri = lax.broadcasted_iota(jnp.int32, (C, C), 0)
    ci = lax.broadcasted_iota(jnp.int32, (C, C), 1)
    lower = (ri >= ci).astype(bf16)
    upper = (ri <= ci).astype(bf16)
    ones = jnp.ones((C, C), bf16)
    eye = (ri == ci).astype(f32)
    incl = (ri >= ci, ri <= ci)
    strict = (ri > ci, ri < ci)
    pair = ([], [])
    blk = 1
    while blk < C:
        same = (ri // (2 * blk)) == (ci // (2 * blk))
        hi_r, hi_c = (ri % (2 * blk)) >= blk, (ci % (2 * blk)) >= blk
        pair[0].append(same & hi_r & jnp.logical_not(hi_c))
        pair[1].append(same & hi_c & jnp.logical_not(hi_r))
        blk *= 2

    col_is_fwd = lax.broadcasted_iota(jnp.int32, (C, 4 * nh), 1) < nh
    row_is_fwd = lax.broadcasted_iota(jnp.int32, (2 * nh, C), 0) < nh
    for t in range(n_chunks):
        if t < n_ctx:
            ab = abc_ref[pl.ds(t * C, C), :]
            abt = abtc_ref[:, t * C:(t + 1) * C]
        else:
            ab = abl_ref[pl.ds((t - n_ctx) * C, C), :]
            abt = abtl_ref[:, (t - n_ctx) * C:(t - n_ctx + 1) * C]
        la = -jnp.exp(alc_ref[...]) * _softplus(ab + dtc_ref[...])
        p3 = _split3(la)
        pre = sum(_dot(lower, p) for p in p3)
        suf = sum(_dot(upper, p) for p in p3)
        gcol_s[t] = jnp.where(col_is_fwd, pre, suf)
        gtcol_s[t] = sum(_dot(ones, p) for p in p3)
        bcol_s[t] = jax.nn.sigmoid(ab)
        lar = -jnp.exp(alr_ref[...]) * _softplus(abt[0:2 * nh, :] + dtr_ref[...])
        r3 = _split3(lar)
        pre_r = sum(_dot(p, upper) for p in r3)
        suf_r = sum(_dot(p, lower) for p in r3)
        grow_s[t] = jnp.where(row_is_fwd, pre_r, suf_r)
        gtrow_s[t] = sum(_dot(p, ones) for p in r3)

    def chunk_step(t, d, h):
        col = d * nh + h
        r0 = pl.multiple_of(t * C, C)
        hs = slice(h * B_HEAD_DIM, (h + 1) * B_HEAD_DIM)
        q = q_s[pl.ds(r0, C), hs]
        k = k_s[pl.ds(r0, C), hs]
        v = v_s[pl.ds(r0, C), hs]
        gc = gcol_s[t][:, col:col + 1]
        gtc = gtcol_s[t][:, col:col + 1]
        bc = bcol_s[t][:, 2 * nh + col:2 * nh + col + 1]
        gr = grow_s[t][col:col + 1, :]
        gtr = gtrow_s[t][col:col + 1, :]
        qk_kk = lax.dot_general(jnp.concatenate([q, k], axis=0), k, NT, preferred_element_type=f32)
        qk, kk = qk_kk[0:C], qk_kk[C:2 * C]
        decay = jnp.where(incl[d], jnp.exp(jnp.where(incl[d], gc - gr, 0.0)), 0.0)
        a = jnp.where(strict[d], bc * kk * decay, 0.0)
        tinv = eye - jnp.where(pair[d][0], a, 0.0)
        for lvl in range(1, len(pair[d])):
            a_off = jnp.where(pair[d][lvl], a, 0.0).astype(bf16)
            tb16 = tinv.astype(bf16)
            x = _dot(tb16, a_off).astype(bf16)
            tinv = tinv - _dot(x, tb16)
        e_gc = jnp.exp(gc)
        qf, kf, vf = q.astype(f32), k.astype(f32), v.astype(f32)
        vb_kbg = jnp.concatenate([vf * bc, kf * (bc * e_gc)], axis=1).astype(bf16)
        uw = _dot(tinv.astype(bf16), vb_kbg)
        u, w = uw[:, 0:B_HEAD_DIM], uw[:, B_HEAD_DIM:]
        attn = (qk * decay).astype(bf16)
        qg = qf * e_gc
        s = st_s[col]
        wq_s = _dot(jnp.concatenate([w, qg], axis=0).astype(bf16), s.astype(bf16))
        v_new = (u - wq_s[0:C]).astype(bf16)
        o = wq_s[C:2 * C] + _dot(attn, v_new)
        k_t = lax.dot_general(eye.astype(bf16), k, NT, preferred_element_type=f32)
        kdec_t = (k_t * jnp.exp(gtr - gr)).astype(bf16)
        st_s[col] = s * jnp.exp(gtc[0:1, :]) + _dot(kdec_t, v_new)
        o_s[pl.ds(r0, C), hs] += o

    def body(i, carry):
        tb = jnp.where(i < n_ctx, n_ctx - 1 - i, n_chunks + n_ctx - 1 - i)
        for h in range(nh):
            chunk_step(i, 0, h)
            chunk_step(tb, 1, h)
        return carry

    lax.fori_loop(0, n_chunks, body, 0)

    ng = ng_ref[...]
    for t in range(n_chunks):
        if t < n_ctx:
            gate, y_ref, rows = gc_ref[pl.ds(t * C, C), :], yc_ref, pl.ds(t * C, C)
        else:
            gate, y_ref, rows = gl_ref[pl.ds((t - n_ctx) * C, C), :], yl_ref, pl.ds((t - n_ctx) * C, C)
        o = o_s[pl.ds(t * C, C), :]
        gf = gate.astype(f32)
        for h in range(nh):
            hs = slice(h * B_HEAD_DIM, (h + 1) * B_HEAD_DIM)
            y_ref[rows, hs] = (_rms(o[:, hs], ng) * _silu(gf[:, hs])).astype(bf16)


def _gdn(ctx_parts, lat_parts, a_log, dt_bias, norm_g):
    gc, qc, kc, vc, abc, abtc = ctx_parts
    gl, ql, kl, vl, abl, abtl = lat_parts
    n_batch, t_ctx, _ = qc.shape
    t_lat = ql.shape[1]
    t_all = t_ctx + t_lat
    n_chunks = t_all // GDN_CHUNK
    nh = B_HEADS
    pad = jnp.zeros((2 * nh,), f32)
    alc = jnp.concatenate([a_log.reshape(-1), pad]).reshape(1, 4 * nh)
    dtc = jnp.concatenate([dt_bias.reshape(-1), pad]).reshape(1, 4 * nh)
    alr = jnp.broadcast_to(a_log.reshape(2 * nh, 1), (2 * nh, GDN_CHUNK))
    dtr = jnp.broadcast_to(dt_bias.reshape(2 * nh, 1), (2 * nh, GDN_CHUNK))
    seg = lambda t, wd: pl.BlockSpec((None, t, wd), lambda b: (b, 0, 0))
    segt = lambda t: pl.BlockSpec((None, 4 * nh, t), lambda b: (b, 0, 0))
    one = lambda shape: pl.BlockSpec(shape, lambda b: (0,) * len(shape))
    in_specs = ([seg(t_ctx, B_WIDTH)] * 4 + [seg(t_ctx, 4 * nh), segt(t_ctx)]
                + [seg(t_lat, B_WIDTH)] * 4 + [seg(t_lat, 4 * nh), segt(t_lat)]
                + [one((1, 4 * nh)), one((1, 4 * nh)), one((2 * nh, GDN_CHUNK)),
                   one((2 * nh, GDN_CHUNK)), one((1, B_HEAD_DIM))])
    return pl.pallas_call(
        functools.partial(_gdn_kernel, t_ctx=t_ctx, t_lat=t_lat),
        out_shape=(jax.ShapeDtypeStruct((n_batch, t_ctx, B_WIDTH), bf16),
                   jax.ShapeDtypeStruct((n_batch, t_lat, B_WIDTH), bf16)),
        grid=(n_batch,),
        in_specs=in_specs,
        out_specs=(seg(t_ctx, B_WIDTH), seg(t_lat, B_WIDTH)),
        scratch_shapes=[
            pltpu.VMEM((t_all, B_WIDTH), bf16), pltpu.VMEM((t_all, B_WIDTH), bf16),
            pltpu.VMEM((t_all, B_WIDTH), bf16), pltpu.VMEM((t_all, B_WIDTH), f32),
            pltpu.VMEM((2 * nh, B_HEAD_DIM, B_HEAD_DIM), f32),
            pltpu.VMEM((n_chunks, GDN_CHUNK, 4 * nh), f32), pltpu.VMEM((n_chunks, GDN_CHUNK, 4 * nh), f32),
            pltpu.VMEM((n_chunks, GDN_CHUNK, 4 * nh), f32),
            pltpu.VMEM((n_chunks, 2 * nh, GDN_CHUNK), f32), pltpu.VMEM((n_chunks, 2 * nh, GDN_CHUNK), f32)],
        compiler_params=_params(1),
        name="gdn",
    )(qc, kc, vc, gc, abc, abtc, ql, kl, vl, gl, abl, abtl, alc, dtc, alr, dtr, norm_g)


def _mix0_kernel(x_ref, auav_ref, yb_ref, lng_ref, lnb_ref, ws_ref, bst_ref, wout_ref, g1_ref,
                 o_ref, mix_ref, *, tm):
    au = auav_ref[:, 0:A_WIDTH].astype(f32)
    av = auav_ref[:, A_WIDTH:2 * A_WIDTH].astype(f32)
    u = _gelu(au)
    gv = _gelu(av)
    mu = jnp.mean(gv, axis=-1, keepdims=True)
    dv = gv - mu
    var = jnp.mean(dv * dv, axis=-1, keepdims=True)
    vn = (dv * lax.rsqrt(var + EPS) * lng_ref[...] + lnb_ref[...]).astype(bf16)
    for c in range(tm // A_CHUNK):
        rows = slice(c * A_CHUNK, (c + 1) * A_CHUNK)
        for g in range(A_GROUPS):
            cols = slice(g * A_GROUP_DIM, (g + 1) * A_GROUP_DIM)
            s = _dot(ws_ref[g], vn[rows, cols]) + bst_ref[:, g:g + 1]
            mix_ref[rows, cols] = (u[rows, cols] * s).astype(bf16)
    mix_ref[:, A_WIDTH:] = yb_ref[...]
    o_ref[...] = x_ref[...] + g1_ref[...] * _dot(mix_ref[...], wout_ref[...])


def _mix0(x, auav, yb, mods, w, is_ctx, tm):
    n_batch, t_len, _ = x.shape
    nb = mods.shape[0] - 1
    return pl.pallas_call(
        functools.partial(_mix0_kernel, tm=tm),
        out_shape=jax.ShapeDtypeStruct(x.shape, f32),
        grid=(n_batch, t_len // tm),
        in_specs=[_tok_spec(tm, D_MODEL), _tok_spec(tm, 2 * A_WIDTH), _tok_spec(tm, B_WIDTH),
                  _full((1, A_WIDTH)), _full((1, A_WIDTH)), _full(w["ws"].shape), _full(w["bst"].shape),
                  _full(w["wout"].shape), _mod_spec(2, is_ctx, nb)],
        out_specs=_tok_spec(tm, D_MODEL),
        scratch_shapes=[pltpu.VMEM((tm, A_WIDTH + B_WIDTH), bf16)],
        compiler_params=_params(2),
        name="mix0_ctx" if is_ctx else "mix0_lat",
    )(x, auav, yb, w["lng"], w["lnb"], w["ws"], w["bst"], w["wout"], mods)


def _ffn_kernel(x_ref, xp_ref, xn_ref, g_ref, sh_ref, sc_ref, gt_ref, wup_ref, conv_ref, wdn_ref, fg_ref,
                o_ref, hs_ref, zs_ref, act_ref, *, tm, final_norm):
    _fill_halo_h(x_ref, xp_ref, xn_ref, g_ref, sh_ref, sc_ref, hs_ref, tm)
    hb_all = hs_ref[...].astype(bf16)
    for j in range(D_FF // FFN_CW):
        cols = slice(2 * j * FFN_CW, 2 * (j + 1) * FFN_CW)
        zs_ref[...] = _dot(hb_all, wup_ref[:, cols])
        y = _conv3(zs_ref, conv_ref[:, cols], tm)
        act_ref[:, j * FFN_CW:(j + 1) * FFN_CW] = (_silu(y[:, 0:FFN_CW]) * y[:, FFN_CW:]).astype(bf16)
    out = x_ref[...] + gt_ref[...] * _dot(act_ref[...], wdn_ref[...])
    if final_norm:
        out = _rms(out, fg_ref[...])
    o_ref[...] = out


def _ffn(x, mods, norm_g, w, final_g, is_ctx, tm, final_norm):
    n_batch, t_len, _ = x.shape
    nb = mods.shape[0] - 1
    return pl.pallas_call(
        functools.partial(_ffn_kernel, tm=tm, final_norm=final_norm),
        out_shape=jax.ShapeDtypeStruct(x.shape, f32),
        grid=(n_batch, t_len // tm),
        in_specs=_halo_specs(tm, t_len) + [
            _full((1, D_MODEL)), _mod_spec(3, is_ctx, nb), _mod_spec(4, is_ctx, nb), _mod_spec(5, is_ctx, nb),
            _full(w["wup"].shape), _full(w["conv"].shape), _full(w["wdn"].shape), _full((1, D_MODEL))],
        out_specs=_tok_spec(tm, D_MODEL),
        scratch_shapes=[pltpu.VMEM((tm + 2 * HALO, D_MODEL), f32),
                        pltpu.VMEM((tm + 2 * HALO, 2 * FFN_CW), f32),
                        pltpu.VMEM((tm, D_FF), bf16)],
        compiler_params=_params(2),
        name="ffn_ctx" if is_ctx else "ffn_lat",
    )(x, x, x, norm_g, mods, mods, mods, w["wup"], w["conv"], w["wdn"], final_g)


def _mla_proj_kernel(*refs, with_q):
    if with_q:
        (x_ref, g_ref, sh_ref, sc_ref, win_ref, qng_ref, kvng_ref, wkn_ref, wv_ref,
         wqn_ref, wqr_ref, wqrs_ref, cos_ref, sin_ref,
         kn_ref, v_ref, kr_ref, qn_ref, qr_ref) = refs
    else:
        (x_ref, g_ref, sh_ref, sc_ref, win_ref, kvng_ref, wkn_ref, wv_ref,
         kn_ref, v_ref, kr_ref) = refs
    hb = _norm_mod(x_ref[...], g_ref[...], sh_ref[...], sc_ref[...]).astype(bf16)
    z = _dot(hb, win_ref[...])
    off = C_Q_LORA if with_q else 0
    ckvn = _rms(z[:, off:off + C_KV_LORA], kvng_ref[...]).astype(bf16)
    kn_ref[...] = _dot(ckvn, wkn_ref[...]).astype(bf16)
    v_ref[...] = _dot(ckvn, wv_ref[...]).astype(bf16)
    kr = z[:, off + C_KV_LORA:off + C_KV_LORA + C_ROPE]
    if not with_q:
        kr_ref[...] = kr.astype(bf16)
        return
    krs = z[:, off + C_KV_LORA + C_ROPE:off + C_KV_LORA + 2 * C_ROPE]
    cos2, sin2 = cos_ref[...], sin_ref[...]
    kr_ref[...] = (kr * cos2[:, 0:C_ROPE] + krs * sin2[:, 0:C_ROPE]).astype(bf16)
    cqn = _rms(z[:, 0:C_Q_LORA], qng_ref[...]).astype(bf16)
    qn_ref[...] = _dot(cqn, wqn_ref[...]).astype(bf16)
    qr = _dot(cqn, wqr_ref[...])
    qrs = _dot(cqn, wqrs_ref[...])
    for hp in range(C_HEADS // 2):
        cols = slice(hp * 2 * C_ROPE, (hp + 1) * 2 * C_ROPE)
        rot = (qr[:, cols] * cos2 + qrs[:, cols] * sin2).astype(bf16)
        qr_ref[2 * hp] = rot[:, 0:C_ROPE]
        qr_ref[2 * hp + 1] = rot[:, C_ROPE:]


def _mla_proj(x, mods, norm_g, w, with_q, tm):
    n_batch, t_len, _ = x.shape
    nb = mods.shape[0] - 1
    is_ctx = not with_q
    hw = C_HEADS * C_NOPE
    tok = lambda width: jax.ShapeDtypeStruct((n_batch, t_len, width), bf16)
    out_shape = [tok(hw), tok(C_HEADS * C_VDIM), tok(C_ROPE)]
    out_specs = [_tok_spec(tm, hw), _tok_spec(tm, C_HEADS * C_VDIM), _tok_spec(tm, C_ROPE)]
    in_specs = [_tok_spec(tm, D_MODEL), _full((1, D_MODEL)), _mod_spec(0, is_ctx, nb), _mod_spec(1, is_ctx, nb)]
    if with_q:
        args = [w["win"], w["qng"], w["kvng"], w["wkn"], w["wv"], w["wqn"], w["wqr"], w["wqrs"]]
        in_specs += [_full(a.shape) for a in args]
        in_specs += [pl.BlockSpec((tm, 2 * C_ROPE), lambda b, i: (i, 0))] * 2
        args += [w["cos2"], w["sin2"]]
        out_shape += [tok(hw), jax.ShapeDtypeStruct((n_batch, C_HEADS, t_len, C_ROPE), bf16)]
        out_specs += [_tok_spec(tm, hw), pl.BlockSpec((None, C_HEADS, tm, C_ROPE), lambda b, i: (b, 0, i, 0))]
    else:
        args = [w["win_kv"], w["kvng"], w["wkn"], w["wv"]]
        in_specs += [_full(a.shape) for a in args]
    return pl.pallas_call(
        functools.partial(_mla_proj_kernel, with_q=with_q),
        out_shape=tuple(out_shape),
        grid=(n_batch, t_len // tm),
        in_specs=in_specs,
        out_specs=tuple(out_specs),
        compiler_params=_params(2),
        name="mla_proj_lat" if with_q else "mla_proj_ctx",
    )(x, norm_g, mods, mods, *args)


def _attn_kernel(qn_ref, qr_ref, knc_ref, krc_ref, vc_ref, knl_ref, krl_ref, vl_ref, o_ref,
                 kc_s, kl_s, q_s, *, scale):
    @pl.when(pl.program_id(2) == 0)
    def _():
        for k_s, kn_ref, kr_ref in ((kc_s, knc_ref, krc_ref), (kl_s, knl_ref, krl_ref)):
            k_s[:, 0:C_NOPE] = kn_ref[...]
            k_s[:, C_NOPE:C_NOPE + C_ROPE] = kr_ref[...]
            k_s[:, C_NOPE + C_ROPE:] = jnp.zeros((k_s.shape[0], C_NOPE - C_ROPE), bf16)

    q_s[:, 0:C_NOPE] = qn_ref[...]
    q_s[:, C_NOPE:C_NOPE + C_ROPE] = qr_ref[...]
    q_s[:, C_NOPE + C_ROPE:] = jnp.zeros((q_s.shape[0], C_NOPE - C_ROPE), bf16)
    q = q_s[...]
    sc = lax.dot_general(q, kc_s[...], NT, preferred_element_type=f32) * scale
    sl = lax.dot_general(q, kl_s[...], NT, preferred_element_type=f32) * scale
    m = jnp.maximum(jnp.max(sc, axis=-1, keepdims=True), jnp.max(sl, axis=-1, keepdims=True))
    pc = jnp.exp(sc - m)
    pl_ = jnp.exp(sl - m)
    denom = jnp.sum(pc, axis=-1, keepdims=True) + jnp.sum(pl_, axis=-1, keepdims=True)
    o = _dot(pc.astype(bf16), vc_ref[...]) + _dot(pl_.astype(bf16), vl_ref[...])
    o_ref[...] = (o / denom).astype(bf16)


def _attention(qn, qr, kn_c, kr_c, v_c, kn_l, kr_l, v_l, tq):
    n_batch, t_lat, _ = qn.shape
    t_ctx = kn_c.shape[1]
    head = lambda t: pl.BlockSpec((None, t, C_NOPE), lambda b, h, i: (b, 0, h))
    rope = lambda t: pl.BlockSpec((None, t, C_ROPE), lambda b, h, i: (b, 0, 0))
    return pl.pallas_call(
        functools.partial(_attn_kernel, scale=(C_NOPE + C_ROPE) ** -0.5),
        out_shape=jax.ShapeDtypeStruct((n_batch, t_lat, C_HEADS * C_VDIM), bf16),
        grid=(n_batch, C_HEADS, t_lat // tq),
        in_specs=[pl.BlockSpec((None, tq, C_NOPE), lambda b, h, i: (b, i, h)),
                  pl.BlockSpec((None, None, tq, C_ROPE), lambda b, h, i: (b, h, i, 0)),
                  head(t_ctx), rope(t_ctx), head(t_ctx), head(t_lat), rope(t_lat), head(t_lat)],
        out_specs=pl.BlockSpec((None, tq, C_VDIM), lambda b, h, i: (b, i, h)),
        scratch_shapes=[pltpu.VMEM((t_ctx, 2 * C_NOPE), bf16), pltpu.VMEM((t_lat, 2 * C_NOPE), bf16),
                        pltpu.VMEM((tq, 2 * C_NOPE), bf16)],
        compiler_params=_params(3),
        name="mla_attention",
    )(qn, qr, kn_c, kr_c, v_c, kn_l, kr_l, v_l)


def _proj_res_kernel(x_ref, y_ref, w_ref, g1_ref, o_ref):
    o_ref[...] = x_ref[...] + g1_ref[...] * _dot(y_ref[...], w_ref[...])


def _proj_res(x, y, w_out, mods, tm):
    n_batch, t_len, _ = x.shape
    nb = mods.shape[0] - 1
    return pl.pallas_call(
        _proj_res_kernel,
        out_shape=jax.ShapeDtypeStruct(x.shape, f32),
        grid=(n_batch, t_len // tm),
        in_specs=[_tok_spec(tm, D_MODEL), _tok_spec(tm, y.shape[-1]), _full(w_out.shape),
                  _mod_spec(2, False, nb)],
        out_specs=_tok_spec(tm, D_MODEL),
        compiler_params=_params(2),
        name="mla_out_proj",
    )(x, y, w_out, mods)


def _rope_tables(n):
    rows = n // GRID_W
    row = jnp.repeat(jnp.arange(rows, dtype=f32), GRID_W)
    col = jnp.tile(jnp.arange(GRID_W, dtype=f32), rows)
    n_freq = C_ROPE // 4
    inv = ROPE_THETA ** (-jnp.arange(n_freq, dtype=f32) / n_freq)
    ang = jnp.concatenate([row[:, None] * inv, col[:, None] * inv], axis=-1)
    cos, sin = jnp.cos(ang), jnp.sin(ang)
    cos64 = jnp.concatenate([cos, cos], axis=-1)
    sin64 = jnp.concatenate([-sin, sin], axis=-1)
    return jnp.tile(cos64, (1, 2)), jnp.tile(sin64, (1, 2))


def _swap_halves(w):
    h = C_ROPE // 2
    return jnp.concatenate([w[..., h:], w[..., :h]], axis=-1)


def _ffn_weights(w_up, conv_w, w_down):
    n = D_FF // FFN_CW
    def inter(a):
        lead = a.shape[0]
        g = a[:, :D_FF].reshape(lead, n, 1, FFN_CW)
        u = a[:, D_FF:].reshape(lead, n, 1, FFN_CW)
        return jnp.concatenate([g, u], axis=2).reshape(lead, 2 * D_FF)
    return {"wup": inter(w_up).astype(bf16), "conv": inter(conv_w), "wdn": w_down.astype(bf16)}


def kernel(x, c, ctx, c_ctx, ada_w, ada_b, norm1_g, norm2_g, ab_w_in, a_ln_g, a_ln_b, a_ws, a_bs, b_conv_w, b_a_log, b_dt_bias, b_norm_g, ab_w_out, mla_w_in, mla_q_norm_g, mla_kv_norm_g, mla_w_uq, mla_w_ukv, mla_w_out, ffn_w_up, ffn_conv_w, ffn_w_down, final_g):
    n_batch, t_lat, d = x.shape
    t_ctx = ctx.shape[1]
    tm_lat, tm_ctx = 512, t_ctx

    cc = jnp.concatenate([c, c_ctx[None, :], jnp.zeros((7, d), f32)], axis=0)
    mods_all = _ada_mods(cc, ada_w, ada_b)
    mods = [mods_all[i, :n_batch + 1].reshape(n_batch + 1, 6, 1, d) for i in range(2)]
    row = lambda v: v.reshape(1, -1)

    a2, b4 = 2 * A_WIDTH, 4 * B_WIDTH
    w_in = ab_w_in[0]
    w_ab = w_in[:, a2 + b4:]
    w0 = {
        "wqkv": w_in[:, a2:a2 + 3 * B_WIDTH].astype(bf16),
        "wrest": jnp.concatenate([w_in[:, :a2], w_in[:, a2 + 3 * B_WIDTH:a2 + b4]], axis=1).astype(bf16),
        "wab": jnp.pad(w_ab, ((0, 0), (0, 128 - 4 * B_HEADS))).astype(bf16),
        "wabt": w_ab.T.astype(bf16),
        "conv": b_conv_w[0],
    }
    parts_c = _inproj0(ctx, mods[0], row(norm1_g[0]), w0, True, tm_ctx)
    parts_l = _inproj0(x, mods[0], row(norm1_g[0]), w0, False, tm_lat)
    yb_c, yb_l = _gdn(parts_c[1:], parts_l[1:], b_a_log[0], b_dt_bias[0], row(b_norm_g[0]))
    wm = {"lng": row(a_ln_g[0]), "lnb": row(a_ln_b[0]), "ws": a_ws[0].astype(bf16), "bst": a_bs[0].T,
          "wout": ab_w_out[0].astype(bf16)}
    cx = _mix0(ctx, parts_c[0], yb_c, mods[0], wm, True, tm_ctx)
    lat = _mix0(x, parts_l[0], yb_l, mods[0], wm, False, tm_lat)
    wf = _ffn_weights(ffn_w_up[0], ffn_conv_w[0], ffn_w_down[0])
    cx = _ffn(cx, mods[0], row(norm2_g[0]), wf, row(final_g), True, tm_ctx, False)
    lat = _ffn(lat, mods[0], row(norm2_g[0]), wf, row(final_g), False, tm_lat, False)

    w_in = mla_w_in[0]
    kv0 = C_Q_LORA
    kr0 = C_Q_LORA + C_KV_LORA
    w_uq = mla_w_uq[0].reshape(C_Q_LORA, C_HEADS, C_NOPE + C_ROPE)
    w_ukv = mla_w_ukv[0].reshape(C_KV_LORA, C_HEADS, C_NOPE + C_VDIM)
    w_qr = w_uq[:, :, C_NOPE:]
    cos2, sin2 = _rope_tables(t_lat)
    w1 = {
        "win": jnp.concatenate([w_in, _swap_halves(w_in[:, kr0:])], axis=1).astype(bf16),
        "win_kv": jnp.pad(w_in[:, kv0:], ((0, 0), (0, C_ROPE))).astype(bf16),
        "qng": row(mla_q_norm_g[0]), "kvng": row(mla_kv_norm_g[0]),
        "wkn": w_ukv[:, :, :C_NOPE].reshape(C_KV_LORA, -1).astype(bf16),
        "wv": w_ukv[:, :, C_NOPE:].reshape(C_KV_LORA, -1).astype(bf16),
        "wqn": w_uq[:, :, :C_NOPE].reshape(C_Q_LORA, -1).astype(bf16),
        "wqr": w_qr.reshape(C_Q_LORA, -1).astype(bf16),
        "wqrs": _swap_halves(w_qr).reshape(C_Q_LORA, -1).astype(bf16),
        "cos2": cos2, "sin2": sin2,
    }
    kn_c, v_c, kr_c = _mla_proj(cx, mods[1], row(norm1_g[1]), w1, False, tm_ctx)
    kn_l, v_l, kr_l, qn, qr = _mla_proj(lat, mods[1], row(norm1_g[1]), w1, True, tm_lat)
    att = _attention(qn, qr, kn_c, kr_c, v_c, kn_l, kr_l, v_l, 512)
    lat = _proj_res(lat, att, mla_w_out[0].astype(bf16), mods[1], tm_lat)
    wf = _ffn_weights(ffn_w_up[1], ffn_conv_w[1], ffn_w_down[1])
    return _ffn(lat, mods[1], row(norm2_g[1]), wf, row(final_g), False, tm_lat, True)
```

```python
import functools

import jax
import jax.numpy as jnp
from jax import lax
from jax.experimental import pallas as pl
from jax.experimental.pallas import tpu as pltpu

f32 = jnp.float32
bf16 = jnp.bfloat16

D_MODEL = 1024
GRID_W = 64
A_GROUPS = 4
A_GROUP_DIM = 128
A_WIDTH = A_GROUPS * A_GROUP_DIM
A_CHUNK = 128
B_HEADS = 4
B_HEAD_DIM = 128
B_WIDTH = B_HEADS * B_HEAD_DIM
C_HEADS = 8
C_NOPE = 128
C_ROPE = 64
C_VDIM = 128
C_Q_LORA = 384
C_KV_LORA = 256
ROPE_THETA = 10000.0
D_FF = 2816
EPS = 1e-6

HALO = 8
GDN_CHUNK = 128
FFN_CW = 256
VMEM_LIMIT = 56 * 1024 * 1024

NT = (((1,), (1,)), ((), ()))
TN = (((0,), (0,)), ((), ()))


def _dot(a, b):
    return jnp.dot(a, b, preferred_element_type=f32)


def _silu(x):
    return x * jax.nn.sigmoid(x)


def _gelu(x):
    return 0.5 * x * (1.0 + lax.erf(x * (0.5 ** 0.5)))


def _softplus(x):
    return jnp.maximum(x, 0.0) + jnp.log1p(jnp.exp(-jnp.abs(x)))


def _rms(x, g):
    return x * lax.rsqrt(jnp.mean(x * x, axis=-1, keepdims=True) + EPS) * g


def _norm_mod(x, g, shift, scale):
    return _rms(x, g) * (1.0 + scale) + shift


def _split3(x):
    hi = x.astype(bf16)
    r = x - hi.astype(f32)
    mid = r.astype(bf16)
    lo = (r - mid.astype(f32)).astype(bf16)
    return hi, mid, lo


def _params(n_axes):
    return pltpu.CompilerParams(dimension_semantics=("arbitrary",) * n_axes,
                                vmem_limit_bytes=VMEM_LIMIT)


def _full(shape):
    nd = len(shape)
    return pl.BlockSpec(shape, lambda *_: (0,) * nd, pipeline_mode=pl.Buffered(1))


def _mod_spec(chunk, is_ctx, n_batch):
    if is_ctx:
        return pl.BlockSpec((None, None, 1, D_MODEL), lambda b, i: (n_batch, chunk, 0, 0))
    return pl.BlockSpec((None, None, 1, D_MODEL), lambda b, i: (b, chunk, 0, 0))


def _tok_spec(tm, width):
    return pl.BlockSpec((None, tm, width), lambda b, i: (b, i, 0))


def _halo_specs(tm, t_len):
    r = tm // HALO
    last = t_len // HALO - 1
    return [
        pl.BlockSpec((None, tm, D_MODEL), lambda b, i: (b, i, 0)),
        pl.BlockSpec((None, HALO, D_MODEL), lambda b, i: (b, jnp.maximum(i * r - 1, 0), 0)),
        pl.BlockSpec((None, HALO, D_MODEL), lambda b, i: (b, jnp.minimum((i + 1) * r, last), 0)),
    ]


def _fill_halo_h(x_ref, xp_ref, xn_ref, g_ref, sh_ref, sc_ref, hs_ref, tm):
    i = pl.program_id(1)
    nt = pl.num_programs(1)
    g, sh, sc = g_ref[...], sh_ref[...], sc_ref[...]
    hs_ref[pl.ds(HALO, tm), :] = _norm_mod(x_ref[...], g, sh, sc)
    hs_ref[pl.ds(0, HALO), :] = jnp.where(i > 0, _norm_mod(xp_ref[...], g, sh, sc), 0.0)
    hs_ref[pl.ds(HALO + tm, HALO), :] = jnp.where(i < nt - 1, _norm_mod(xn_ref[...], g, sh, sc), 0.0)


def _conv3(z_ref, cw, tm):
    return (cw[0:1] * z_ref[pl.ds(HALO - 1, tm), :] + cw[1:2] * z_ref[pl.ds(HALO, tm), :]
            + cw[2:3] * z_ref[pl.ds(HALO + 1, tm), :])


def _ada_kernel(c_ref, w_ref, b_ref, o_ref):
    s = _silu(c_ref[...]).astype(bf16)
    o_ref[...] = _dot(s, w_ref[...].astype(bf16)) + b_ref[...]


def _ada_mods(cc, ada_w, ada_b):
    depth = ada_w.shape[0]
    rows = cc.shape[0]
    return pl.pallas_call(
        _ada_kernel,
        out_shape=jax.ShapeDtypeStruct((depth, rows, 6 * D_MODEL), f32),
        grid=(depth, 6),
        in_specs=[pl.BlockSpec((rows, D_MODEL), lambda l, j: (0, 0)),
                  pl.BlockSpec((None, D_MODEL, D_MODEL), lambda l, j: (l, 0, j)),
                  pl.BlockSpec((None, 1, D_MODEL), lambda l, j: (l, 0, j))],
        out_specs=pl.BlockSpec((None, rows, D_MODEL), lambda l, j: (l, 0, j)),
        compiler_params=_params(2),
        name="ada_mods",
    )(cc, ada_w, ada_b.reshape(depth, 1, 6 * D_MODEL))


def _inproj0_kernel(x_ref, xp_ref, xn_ref, g_ref, sh_ref, sc_ref, wqkv_ref, wrest_ref, wab_ref,
                    wabt_ref, conv_ref, auav_ref, gate_ref, q_ref, k_ref, v_ref, ab_ref, abt_ref,
                    hs_ref, zs_ref, *, tm):
    _fill_halo_h(x_ref, xp_ref, xn_ref, g_ref, sh_ref, sc_ref, hs_ref, tm)
    hb_all = hs_ref[...].astype(bf16)
    hb = hs_ref[pl.ds(HALO, tm), :].astype(bf16)
    auav_ref[:, 0:A_WIDTH] = _dot(hb, wrest_ref[:, 0:A_WIDTH]).astype(bf16)
    auav_ref[:, A_WIDTH:2 * A_WIDTH] = _dot(hb, wrest_ref[:, A_WIDTH:2 * A_WIDTH]).astype(bf16)
    gate_ref[...] = _dot(hb, wrest_ref[:, 2 * A_WIDTH:]).astype(bf16)
    ab_ref[...] = _dot(hb, wab_ref[...])[:, 0:4 * B_HEADS]
    abt_ref[...] = lax.dot_general(wabt_ref[...], hb, NT, preferred_element_type=f32)
    for j, o_ref in enumerate((q_ref, k_ref, v_ref)):
        cols = slice(j * B_WIDTH, (j + 1) * B_WIDTH)
        zs_ref[...] = _dot(hb_all, wqkv_ref[:, cols])
        y = _silu(_conv3(zs_ref, conv_ref[:, cols], tm))
        if j == 2:
            o_ref[...] = y.astype(bf16)
            continue
        post = B_HEAD_DIM ** -0.5 if j == 0 else 1.0
        for h in range(B_HEADS):
            yh = y[:, h * B_HEAD_DIM:(h + 1) * B_HEAD_DIM]
            inv = lax.rsqrt(jnp.sum(yh * yh, axis=-1, keepdims=True) + EPS) * post
            o_ref[:, h * B_HEAD_DIM:(h + 1) * B_HEAD_DIM] = (yh * inv).astype(bf16)


def _inproj0(x, mods, norm_g, w, is_ctx, tm):
    n_batch, t_len, _ = x.shape
    nb = mods.shape[0] - 1
    tok = lambda width, dt: jax.ShapeDtypeStruct((n_batch, t_len, width), dt)
    return pl.pallas_call(
        functools.partial(_inproj0_kernel, tm=tm),
        out_shape=(tok(2 * A_WIDTH, bf16), tok(B_WIDTH, bf16), tok(B_WIDTH, bf16), tok(B_WIDTH, bf16),
                   tok(B_WIDTH, bf16), tok(4 * B_HEADS, f32),
                   jax.ShapeDtypeStruct((n_batch, 4 * B_HEADS, t_len), f32)),
        grid=(n_batch, t_len // tm),
        in_specs=_halo_specs(tm, t_len) + [
            _full((1, D_MODEL)), _mod_spec(0, is_ctx, nb), _mod_spec(1, is_ctx, nb),
            _full(w["wqkv"].shape), _full(w["wrest"].shape), _full(w["wab"].shape),
            _full(w["wabt"].shape), _full(w["conv"].shape)],
        out_specs=(_tok_spec(tm, 2 * A_WIDTH), _tok_spec(tm, B_WIDTH), _tok_spec(tm, B_WIDTH),
                   _tok_spec(tm, B_WIDTH), _tok_spec(tm, B_WIDTH), _tok_spec(tm, 4 * B_HEADS),
                   pl.BlockSpec((None, 4 * B_HEADS, tm), lambda b, i: (b, 0, i))),
        scratch_shapes=[pltpu.VMEM((tm + 2 * HALO, D_MODEL), f32),
                        pltpu.VMEM((tm + 2 * HALO, B_WIDTH), f32)],
        compiler_params=_params(2),
        name="inproj0_ctx" if is_ctx else "inproj0_lat",
    )(x, x, x, norm_g, mods, mods, w["wqkv"], w["wrest"], w["wab"], w["wabt"], w["conv"])


def _gdn_kernel(qc_ref, kc_ref, vc_ref, gc_ref, abc_ref, abtc_ref,
                ql_ref, kl_ref, vl_ref, gl_ref, abl_ref, abtl_ref,
                alc_ref, dtc_ref, alr_ref, dtr_ref, ng_ref,
                yc_ref, yl_ref,
                q_s, k_s, v_s, o_s, st_s, gcol_s, bcol_s, gtcol_s, grow_s, gtrow_s,
                *, t_ctx, t_lat):
    C = GDN_CHUNK
    n_ctx, n_lat = t_ctx // C, t_lat // C
    n_chunks = n_ctx + n_lat
    nh = B_HEADS

    q_s[pl.ds(0, t_ctx), :] = qc_ref[...]
    q_s[pl.ds(t_ctx, t_lat), :] = ql_ref[...]
    k_s[pl.ds(0, t_ctx), :] = kc_ref[...]
    k_s[pl.ds(t_ctx, t_lat), :] = kl_ref[...]
    v_s[pl.ds(0, t_ctx), :] = vc_ref[...]
    v_s[pl.ds(t_ctx, t_lat), :] = vl_ref[...]
    o_s[...] = jnp.zeros_like(o_s)
    st_s[...] = jnp.zeros_like(st_s)

    ri = lax.broadcasted_iota(jnp.int32, (C, C), 0)
    ci = lax.broadcasted_iota(jnp.int32, (C, C), 1)
    lower = (ri >= ci).astype(bf16)
    upper = (ri <= ci).astype(bf16)
    ones = jnp.ones((C, C), bf16)
    eye = (ri == ci).astype(f32)
    incl = (ri >= ci, ri <= ci)
    strict = (ri > ci, ri < ci)
    pair = ([], [])
    blk = 1
    while blk < C:
        same = (ri // (2 * blk)) == (ci // (2 * blk))
        hi_r, hi_c = (ri % (2 * blk)) >= blk, (ci % (2 * blk)) >= blk
        pair[0].append(same & hi_r & jnp.logical_not(hi_c))
        pair[1].append(same & hi_c & jnp.logical_not(hi_r))
        blk *= 2

    col_is_fwd = lax.broadcasted_iota(jnp.int32, (C, 4 * nh), 1) < nh
    row_is_fwd = lax.broadcasted_iota(jnp.int32, (2 * nh, C), 0) < nh
    for t in range(n_chunks):
        if t < n_ctx:
            ab = abc_ref[pl.ds(t * C, C), :]
            abt = abtc_ref[:, t * C:(t + 1) * C]
        else:
            ab = abl_ref[pl.ds((t - n_ctx) * C, C), :]
            abt = abtl_ref[:, (t - n_ctx) * C:(t - n_ctx + 1) * C]
        la = -jnp.exp(alc_ref[...]) * _softplus(ab + dtc_ref[...])
        p3 = _split3(la)
        pre = sum(_dot(lower, p) for p in p3)
        suf = sum(_dot(upper, p) for p in p3)
        gcol_s[t] = jnp.where(col_is_fwd, pre, suf)
        gtcol_s[t] = sum(_dot(ones, p) for p in p3)
        bcol_s[t] = jax.nn.sigmoid(ab)
        lar = -jnp.exp(alr_ref[...]) * _softplus(abt[0:2 * nh, :] + dtr_ref[...])
        r3 = _split3(lar)
        pre_r = sum(_dot(p, upper) for p in r3)
        suf_r = sum(_dot(p, lower) for p in r3)
        grow_s[t] = jnp.where(row_is_fwd, pre_r, suf_r)
        gtrow_s[t] = sum(_dot(p, ones) for p in r3)

    eye16 = eye.astype(bf16)

    def chunk_load(t, d, h):
        col = d * nh + h
        r0 = pl.multiple_of(t * C, C)
        hs = slice(h * B_HEAD_DIM, (h + 1) * B_HEAD_DIM)
        return dict(
            q=q_s[pl.ds(r0, C), hs], k=k_s[pl.ds(r0, C), hs], v=v_s[pl.ds(r0, C), hs],
            gc=gcol_s[t][:, col:col + 1], gtc=gtcol_s[t][:, col:col + 1],
            bc=bcol_s[t][:, 2 * nh + col:2 * nh + col + 1],
            gr=grow_s[t][col:col + 1, :], gtr=gtrow_s[t][col:col + 1, :],
            s=st_s[col], o=o_s[pl.ds(r0, C), hs])

    def chunk_compute(x, d):
        q, k, v, gc, gtc, bc, gr, gtr, s = (x[n] for n in ("q", "k", "v", "gc", "gtc", "bc", "gr", "gtr", "s"))
        qk_kk = lax.dot_general(jnp.concatenate([q, k], axis=0), k, NT, preferred_element_type=f32)
        k_t = lax.dot_general(eye16, k, NT, preferred_element_type=f32)
        yield
        qk, kk = qk_kk[0:C], qk_kk[C:2 * C]
        decay = jnp.where(incl[d], jnp.exp(jnp.where(incl[d], gc - gr, 0.0)), 0.0)
        a = jnp.where(strict[d], bc * kk * decay, 0.0)
        tinv = eye - jnp.where(pair[d][0], a, 0.0)
        for lvl in range(1, len(pair[d])):
            a_off = jnp.where(pair[d][lvl], a, 0.0).astype(bf16)
            tb16 = tinv.astype(bf16)
            ta = _dot(tb16, a_off).astype(bf16)
            yield
            tinv = tinv - _dot(ta, tb16)
            yield
        e_gc = jnp.exp(gc)
        qf, kf, vf = q.astype(f32), k.astype(f32), v.astype(f32)
        vb_kbg = jnp.concatenate([vf * bc, kf * (bc * e_gc)], axis=1).astype(bf16)
        uw = _dot(tinv.astype(bf16), vb_kbg)
        yield
        u, w = uw[:, 0:B_HEAD_DIM], uw[:, B_HEAD_DIM:]
        attn = (qk * decay).astype(bf16)
        wq_s = _dot(jnp.concatenate([w, qf * e_gc], axis=0).astype(bf16), s.astype(bf16))
        yield
        v_new = (u - wq_s[0:C]).astype(bf16)
        o_new = x["o"] + wq_s[C:2 * C] + _dot(attn, v_new)
        kdec_t = (k_t * jnp.exp(gtr - gr)).astype(bf16)
        s_new = s * jnp.exp(gtc[0:1, :]) + _dot(kdec_t, v_new)
        yield s_new, o_new

    def body(i, carry):
        tb = jnp.where(i < n_ctx, n_ctx - 1 - i, n_chunks + n_ctx - 1 - i)
        systems = [(t, d, h) for h in range(nh) for d, t in ((0, i), (1, tb))]
        gens = [chunk_compute(chunk_load(t, d, h), d) for t, d, h in systems]
        while True:
            stage = [next(g) for g in gens]
            if stage[0] is not None:
                break
        for (t, d, h), (s_new, o_new) in zip(systems, stage):
            st_s[d * nh + h] = s_new
            o_s[pl.ds(pl.multiple_of(t * C, C), C), h * B_HEAD_DIM:(h + 1) * B_HEAD_DIM] = o_new
        return carry

    lax.fori_loop(0, n_chunks, body, 0)

    ng = ng_ref[...]
    for t in range(n_chunks):
        if t < n_ctx:
            gate, y_ref, rows = gc_ref[pl.ds(t * C, C), :], yc_ref, pl.ds(t * C, C)
        else:
            gate, y_ref, rows = gl_ref[pl.ds((t - n_ctx) * C, C), :], yl_ref, pl.ds((t - n_ctx) * C, C)
        o = o_s[pl.ds(t * C, C), :]
        gf = gate.astype(f32)
        for h in range(nh):
            hs = slice(h * B_HEAD_DIM, (h + 1) * B_HEAD_DIM)
            y_ref[rows, hs] = (_rms(o[:, hs], ng) * _silu(gf[:, hs])).astype(bf16)


def _gdn(ctx_parts, lat_parts, a_log, dt_bias, norm_g):
    gc, qc, kc, vc, abc, abtc = ctx_parts
    gl, ql, kl, vl, abl, abtl = lat_parts
    n_batch, t_ctx, _ = qc.shape
    t_lat = ql.shape[1]
    t_all = t_ctx + t_lat
    n_chunks = t_all // GDN_CHUNK
    nh = B_HEADS
    pad = jnp.zeros((2 * nh,), f32)
    alc = jnp.concatenate([a_log.reshape(-1), pad]).reshape(1, 4 * nh)
    dtc = jnp.concatenate([dt_bias.reshape(-1), pad]).reshape(1, 4 * nh)
    alr = jnp.broadcast_to(a_log.reshape(2 * nh, 1), (2 * nh, GDN_CHUNK))
    dtr = jnp.broadcast_to(dt_bias.reshape(2 * nh, 1), (2 * nh, GDN_CHUNK))
    seg = lambda t, wd: pl.BlockSpec((None, t, wd), lambda b: (b, 0, 0))
    segt = lambda t: pl.BlockSpec((None, 4 * nh, t), lambda b: (b, 0, 0))
    one = lambda shape: pl.BlockSpec(shape, lambda b: (0,) * len(shape))
    in_specs = ([seg(t_ctx, B_WIDTH)] * 4 + [seg(t_ctx, 4 * nh), segt(t_ctx)]
                + [seg(t_lat, B_WIDTH)] * 4 + [seg(t_lat, 4 * nh), segt(t_lat)]
                + [one((1, 4 * nh)), one((1, 4 * nh)), one((2 * nh, GDN_CHUNK)),
                   one((2 * nh, GDN_CHUNK)), one((1, B_HEAD_DIM))])
    return pl.pallas_call(
        functools.partial(_gdn_kernel, t_ctx=t_ctx, t_lat=t_lat),
        out_shape=(jax.ShapeDtypeStruct((n_batch, t_ctx, B_WIDTH), bf16),
                   jax.ShapeDtypeStruct((n_batch, t_lat, B_WIDTH), bf16)),
        grid=(n_batch,),
        in_specs=in_specs,
        out_specs=(seg(t_ctx, B_WIDTH), seg(t_lat, B_WIDTH)),
        scratch_shapes=[
            pltpu.VMEM((t_all, B_WIDTH), bf16), pltpu.VMEM((t_all, B_WIDTH), bf16),
            pltpu.VMEM((t_all, B_WIDTH), bf16), pltpu.VMEM((t_all, B_WIDTH), f32),
            pltpu.VMEM((2 * nh, B_HEAD_DIM, B_HEAD_DIM), f32),
            pltpu.VMEM((n_chunks, GDN_CHUNK, 4 * nh), f32), pltpu.VMEM((n_chunks, GDN_CHUNK, 4 * nh), f32),
            pltpu.VMEM((n_chunks, GDN_CHUNK, 4 * nh), f32),
            pltpu.VMEM((n_chunks, 2 * nh, GDN_CHUNK), f32), pltpu.VMEM((n_chunks, 2 * nh, GDN_CHUNK), f32)],
        compiler_params=_params(1),
        name="gdn",
    )(qc, kc, vc, gc, abc, abtc, ql, kl, vl, gl, abl, abtl, alc, dtc, alr, dtr, norm_g)


def _mix0_kernel(x_ref, auav_ref, yb_ref, lng_ref, lnb_ref, ws_ref, bst_ref, wout_ref, g1_ref,
                 o_ref, mix_ref, *, tm):
    au = auav_ref[:, 0:A_WIDTH].astype(f32)
    av = auav_ref[:, A_WIDTH:2 * A_WIDTH].astype(f32)
    u = _gelu(au)
    gv = _gelu(av)
    mu = jnp.mean(gv, axis=-1, keepdims=True)
    dv = gv - mu
    var = jnp.mean(dv * dv, axis=-1, keepdims=True)
    vn = (dv * lax.rsqrt(var + EPS) * lng_ref[...] + lnb_ref[...]).astype(bf16)
    for c in range(tm // A_CHUNK):
        rows = slice(c * A_CHUNK, (c + 1) * A_CHUNK)
        for g in range(A_GROUPS):
            cols = slice(g * A_GROUP_DIM, (g + 1) * A_GROUP_DIM)
            s = _dot(ws_ref[g], vn[rows, cols]) + bst_ref[:, g:g + 1]
            mix_ref[rows, cols] = (u[rows, cols] * s).astype(bf16)
    mix_ref[:, A_WIDTH:] = yb_ref[...]
    o_ref[...] = x_ref[...] + g1_ref[...] * _dot(mix_ref[...], wout_ref[...])


def _mix0(x, auav, yb, mods, w, is_ctx, tm):
    n_batch, t_len, _ = x.shape
    nb = mods.shape[0] - 1
    return pl.pallas_call(
        functools.partial(_mix0_kernel, tm=tm),
        out_shape=jax.ShapeDtypeStruct(x.shape, f32),
        grid=(n_batch, t_len // tm),
        in_specs=[_tok_spec(tm, D_MODEL), _tok_spec(tm, 2 * A_WIDTH), _tok_spec(tm, B_WIDTH),
                  _full((1, A_WIDTH)), _full((1, A_WIDTH)), _full(w["ws"].shape), _full(w["bst"].shape),
                  _full(w["wout"].shape), _mod_spec(2, is_ctx, nb)],
        out_specs=_tok_spec(tm, D_MODEL),
        scratch_shapes=[pltpu.VMEM((tm, A_WIDTH + B_WIDTH), bf16)],
        compiler_params=_params(2),
        name="mix0_ctx" if is_ctx else "mix0_lat",
    )(x, auav, yb, w["lng"], w["lnb"], w["ws"], w["bst"], w["wout"], mods)


def _ffn_kernel(x_ref, xp_ref, xn_ref, g_ref, sh_ref, sc_ref, gt_ref, wup_ref, conv_ref, wdn_ref, fg_ref,
                o_ref, hs_ref, zs_ref, act_ref, *, tm, final_norm):
    _fill_halo_h(x_ref, xp_ref, xn_ref, g_ref, sh_ref, sc_ref, hs_ref, tm)
    hb_all = hs_ref[...].astype(bf16)
    for j in range(D_FF // FFN_CW):
        cols = slice(2 * j * FFN_CW, 2 * (j + 1) * FFN_CW)
        zs_ref[...] = _dot(hb_all, wup_ref[:, cols])
        y = _conv3(zs_ref, conv_ref[:, cols], tm)
        act_ref[:, j * FFN_CW:(j + 1) * FFN_CW] = (_silu(y[:, 0:FFN_CW]) * y[:, FFN_CW:]).astype(bf16)
    out = x_ref[...] + gt_ref[...] * _dot(act_ref[...], wdn_ref[...])
    if final_norm:
        out = _rms(out, fg_ref[...])
    o_ref[...] = out


def _ffn(x, mods, norm_g, w, final_g, is_ctx, tm, final_norm):
    n_batch, t_len, _ = x.shape
    nb = mods.shape[0] - 1
    return pl.pallas_call(
        functools.partial(_ffn_kernel, tm=tm, final_norm=final_norm),
        out_shape=jax.ShapeDtypeStruct(x.shape, f32),
        grid=(n_batch, t_len // tm),
        in_specs=_halo_specs(tm, t_len) + [
            _full((1, D_MODEL)), _mod_spec(3, is_ctx, nb), _mod_spec(4, is_ctx, nb), _mod_spec(5, is_ctx, nb),
            _full(w["wup"].shape), _full(w["conv"].shape), _full(w["wdn"].shape), _full((1, D_MODEL))],
        out_specs=_tok_spec(tm, D_MODEL),
        scratch_shapes=[pltpu.VMEM((tm + 2 * HALO, D_MODEL), f32),
                        pltpu.VMEM((tm + 2 * HALO, 2 * FFN_CW), f32),
                        pltpu.VMEM((tm, D_FF), bf16)],
        compiler_params=_params(2),
        name="ffn_ctx" if is_ctx else "ffn_lat",
    )(x, x, x, norm_g, mods, mods, mods, w["wup"], w["conv"], w["wdn"], final_g)


def _mla_proj_kernel(*refs, with_q):
    if with_q:
        (x_ref, g_ref, sh_ref, sc_ref, win_ref, qng_ref, kvng_ref, wkn_ref, wv_ref,
         wqn_ref, wqr_ref, wqrs_ref, cos_ref, sin_ref,
         kn_ref, v_ref, kr_ref, qn_ref, qr_ref) = refs
    else:
        (x_ref, g_ref, sh_ref, sc_ref, win_ref, kvng_ref, wkn_ref, wv_ref,
         kn_ref, v_ref, kr_ref) = refs
    hb = _norm_mod(x_ref[...], g_ref[...], sh_ref[...], sc_ref[...]).astype(bf16)
    z = _dot(hb, win_ref[...])
    off = C_Q_LORA if with_q else 0
    ckvn = _rms(z[:, off:off + C_KV_LORA], kvng_ref[...]).astype(bf16)
    kn_ref[...] = _dot(ckvn, wkn_ref[...]).astype(bf16)
    v_ref[...] = _dot(ckvn, wv_ref[...]).astype(bf16)
    kr = z[:, off + C_KV_LORA:off + C_KV_LORA + C_ROPE]
    if not with_q:
        kr_ref[...] = kr.astype(bf16)
        return
    krs = z[:, off + C_KV_LORA + C_ROPE:off + C_KV_LORA + 2 * C_ROPE]
    cos2, sin2 = cos_ref[...], sin_ref[...]
    kr_ref[...] = (kr * cos2[:, 0:C_ROPE] + krs * sin2[:, 0:C_ROPE]).astype(bf16)
    cqn = _rms(z[:, 0:C_Q_LORA], qng_ref[...]).astype(bf16)
    qn_ref[...] = _dot(cqn, wqn_ref[...]).astype(bf16)
    qr = _dot(cqn, wqr_ref[...])
    qrs = _dot(cqn, wqrs_ref[...])
    for hp in range(C_HEADS // 2):
        cols = slice(hp * 2 * C_ROPE, (hp + 1) * 2 * C_ROPE)
        rot = (qr[:, cols] * cos2 + qrs[:, cols] * sin2).astype(bf16)
        qr_ref[2 * hp] = rot[:, 0:C_ROPE]
        qr_ref[2 * hp + 1] = rot[:, C_ROPE:]


def _mla_proj(x, mods, norm_g, w, with_q, tm):
    n_batch, t_len, _ = x.shape
    nb = mods.shape[0] - 1
    is_ctx = not with_q
    hw = C_HEADS * C_NOPE
    tok = lambda width: jax.ShapeDtypeStruct((n_batch, t_len, width), bf16)
    out_shape = [tok(hw), tok(C_HEADS * C_VDIM), tok(C_ROPE)]
    out_specs = [_tok_spec(tm, hw), _tok_spec(tm, C_HEADS * C_VDIM), _tok_spec(tm, C_ROPE)]
    in_specs = [_tok_spec(tm, D_MODEL), _full((1, D_MODEL)), _mod_spec(0, is_ctx, nb), _mod_spec(1, is_ctx, nb)]
    if with_q:
        args = [w["win"], w["qng"], w["kvng"], w["wkn"], w["wv"], w["wqn"], w["wqr"], w["wqrs"]]
        in_specs += [_full(a.shape) for a in args]
        in_specs += [pl.BlockSpec((tm, 2 * C_ROPE), lambda b, i: (i, 0))] * 2
        args += [w["cos2"], w["sin2"]]
        out_shape += [tok(hw), jax.ShapeDtypeStruct((n_batch, C_HEADS, t_len, C_ROPE), bf16)]
        out_specs += [_tok_spec(tm, hw), pl.BlockSpec((None, C_HEADS, tm, C_ROPE), lambda b, i: (b, 0, i, 0))]
    else:
        args = [w["win_kv"], w["kvng"], w["wkn"], w["wv"]]
        in_specs += [_full(a.shape) for a in args]
    return pl.pallas_call(
        functools.partial(_mla_proj_kernel, with_q=with_q),
        out_shape=tuple(out_shape),
        grid=(n_batch, t_len // tm),
        in_specs=in_specs,
        out_specs=tuple(out_specs),
        compiler_params=_params(2),
        name="mla_proj_lat" if with_q else "mla_proj_ctx",
    )(x, norm_g, mods, mods, *args)


def _attn_kernel(qn_ref, qr_ref, knc_ref, krc_ref, vc_ref, knl_ref, krl_ref, vl_ref, o_ref,
                 kc_s, kl_s, q_s, *, scale):
    @pl.when(pl.program_id(2) == 0)
    def _():
        for k_s, kn_ref, kr_ref in ((kc_s, knc_ref, krc_ref), (kl_s, knl_ref, krl_ref)):
            k_s[:, 0:C_NOPE] = kn_ref[...]
            k_s[:, C_NOPE:C_NOPE + C_ROPE] = kr_ref[...]
            k_s[:, C_NOPE + C_ROPE:] = jnp.zeros((k_s.shape[0], C_NOPE - C_ROPE), bf16)

    q_s[:, 0:C_NOPE] = qn_ref[...]
    q_s[:, C_NOPE:C_NOPE + C_ROPE] = qr_ref[...]
    q_s[:, C_NOPE + C_ROPE:] = jnp.zeros((q_s.shape[0], C_NOPE - C_ROPE), bf16)
    q = q_s[...]
    sc = lax.dot_general(q, kc_s[...], NT, preferred_element_type=f32) * scale
    sl = lax.dot_general(q, kl_s[...], NT, preferred_element_type=f32) * scale
    m = jnp.maximum(jnp.max(sc, axis=-1, keepdims=True), jnp.max(sl, axis=-1, keepdims=True))
    pc = jnp.exp(sc - m)
    pl_ = jnp.exp(sl - m)
    denom = jnp.sum(pc, axis=-1, keepdims=True) + jnp.sum(pl_, axis=-1, keepdims=True)
    o = _dot(pc.astype(bf16), vc_ref[...]) + _dot(pl_.astype(bf16), vl_ref[...])
    o_ref[...] = (o / denom).astype(bf16)


def _attention(qn, qr, kn_c, kr_c, v_c, kn_l, kr_l, v_l, tq):
    n_batch, t_lat, _ = qn.shape
    t_ctx = kn_c.shape[1]
    head = lambda t: pl.BlockSpec((None, t, C_NOPE), lambda b, h, i: (b, 0, h))
    rope = lambda t: pl.BlockSpec((None, t, C_ROPE), lambda b, h, i: (b, 0, 0))
    return pl.pallas_call(
        functools.partial(_attn_kernel, scale=(C_NOPE + C_ROPE) ** -0.5),
        out_shape=jax.ShapeDtypeStruct((n_batch, t_lat, C_HEADS * C_VDIM), bf16),
        grid=(n_batch, C_HEADS, t_lat // tq),
        in_specs=[pl.BlockSpec((None, tq, C_NOPE), lambda b, h, i: (b, i, h)),
                  pl.BlockSpec((None, None, tq, C_ROPE), lambda b, h, i: (b, h, i, 0)),
                  head(t_ctx), rope(t_ctx), head(t_ctx), head(t_lat), rope(t_lat), head(t_lat)],
        out_specs=pl.BlockSpec((None, tq, C_VDIM), lambda b, h, i: (b, i, h)),
        scratch_shapes=[pltpu.VMEM((t_ctx, 2 * C_NOPE), bf16), pltpu.VMEM((t_lat, 2 * C_NOPE), bf16),
                        pltpu.VMEM((tq, 2 * C_NOPE), bf16)],
        compiler_params=_params(3),
        name="mla_attention",
    )(qn, qr, kn_c, kr_c, v_c, kn_l, kr_l, v_l)


def _proj_res_kernel(x_ref, y_ref, w_ref, g1_ref, o_ref):
    o_ref[...] = x_ref[...] + g1_ref[...] * _dot(y_ref[...], w_ref[...])


def _proj_res(x, y, w_out, mods, tm):
    n_batch, t_len, _ = x.shape
    nb = mods.shape[0] - 1
    return pl.pallas_call(
        _proj_res_kernel,
        out_shape=jax.ShapeDtypeStruct(x.shape, f32),
        grid=(n_batch, t_len // tm),
        in_specs=[_tok_spec(tm, D_MODEL), _tok_spec(tm, y.shape[-1]), _full(w_out.shape),
                  _mod_spec(2, False, nb)],
        out_specs=_tok_spec(tm, D_MODEL),
        compiler_params=_params(2),
        name="mla_out_proj",
    )(x, y, w_out, mods)


def _rope_tables(n):
    rows = n // GRID_W
    row = jnp.repeat(jnp.arange(rows, dtype=f32), GRID_W)
    col = jnp.tile(jnp.arange(GRID_W, dtype=f32), rows)
    n_freq = C_ROPE // 4
    inv = ROPE_THETA ** (-jnp.arange(n_freq, dtype=f32) / n_freq)
    ang = jnp.concatenate([row[:, None] * inv, col[:, None] * inv], axis=-1)
    cos, sin = jnp.cos(ang), jnp.sin(ang)
    cos64 = jnp.concatenate([cos, cos], axis=-1)
    sin64 = jnp.concatenate([-sin, sin], axis=-1)
    return jnp.tile(cos64, (1, 2)), jnp.tile(sin64, (1, 2))


def _swap_halves(w):
    h = C_ROPE // 2
    return jnp.concatenate([w[..., h:], w[..., :h]], axis=-1)


def _ffn_weights(w_up, conv_w, w_down):
    n = D_FF // FFN_CW
    def inter(a):
        lead = a.shape[0]
        g = a[:, :D_FF].reshape(lead, n, 1, FFN_CW)
        u = a[:, D_FF:].reshape(lead, n, 1, FFN_CW)
        return jnp.concatenate([g, u], axis=2).reshape(lead, 2 * D_FF)
    return {"wup": inter(w_up).astype(bf16), "conv": inter(conv_w), "wdn": w_down.astype(bf16)}


def kernel(x, c, ctx, c_ctx, ada_w, ada_b, norm1_g, norm2_g, ab_w_in, a_ln_g, a_ln_b, a_ws, a_bs, b_conv_w, b_a_log, b_dt_bias, b_norm_g, ab_w_out, mla_w_in, mla_q_norm_g, mla_kv_norm_g, mla_w_uq, mla_w_ukv, mla_w_out, ffn_w_up, ffn_conv_w, ffn_w_down, final_g):
    n_batch, t_lat, d = x.shape
    t_ctx = ctx.shape[1]
    tm_lat, tm_ctx = 512, t_ctx

    cc = jnp.concatenate([c, c_ctx[None, :], jnp.zeros((7, d), f32)], axis=0)
    mods_all = _ada_mods(cc, ada_w, ada_b)
    mods = [mods_all[i, :n_batch + 1].reshape(n_batch + 1, 6, 1, d) for i in range(2)]
    row = lambda v: v.reshape(1, -1)

    a2, b4 = 2 * A_WIDTH, 4 * B_WIDTH
    w_in = ab_w_in[0]
    w_ab = w_in[:, a2 + b4:]
    w0 = {
        "wqkv": w_in[:, a2:a2 + 3 * B_WIDTH].astype(bf16),
        "wrest": jnp.concatenate([w_in[:, :a2], w_in[:, a2 + 3 * B_WIDTH:a2 + b4]], axis=1).astype(bf16),
        "wab": jnp.pad(w_ab, ((0, 0), (0, 128 - 4 * B_HEADS))).astype(bf16),
        "wabt": w_ab.T.astype(bf16),
        "conv": b_conv_w[0],
    }
    parts_c = _inproj0(ctx, mods[0], row(norm1_g[0]), w0, True, tm_ctx)
    parts_l = _inproj0(x, mods[0], row(norm1_g[0]), w0, False, tm_lat)
    yb_c, yb_l = _gdn(parts_c[1:], parts_l[1:], b_a_log[0], b_dt_bias[0], row(b_norm_g[0]))
    wm = {"lng": row(a_ln_g[0]), "lnb": row(a_ln_b[0]), "ws": a_ws[0].astype(bf16), "bst": a_bs[0].T,
          "wout": ab_w_out[0].astype(bf16)}
    cx = _mix0(ctx, parts_c[0], yb_c, mods[0], wm, True, tm_ctx)
    lat = _mix0(x, parts_l[0], yb_l, mods[0], wm, False, tm_lat)
    wf = _ffn_weights(ffn_w_up[0], ffn_conv_w[0], ffn_w_down[0])
    cx = _ffn(cx, mods[0], row(norm2_g[0]), wf, row(final_g), True, tm_ctx, False)
    lat = _ffn(lat, mods[0], row(norm2_g[0]), wf, row(final_g), False, tm_lat, False)

    w_in = mla_w_in[0]
    kv0 = C_Q_LORA
    kr0 = C_Q_LORA + C_KV_LORA
    w_uq = mla_w_uq[0].reshape(C_Q_LORA, C_HEADS, C_NOPE + C_ROPE)
    w_ukv = mla_w_ukv[0].reshape(C_KV_LORA, C_HEADS, C_NOPE + C_VDIM)
    w_qr = w_uq[:, :, C_NOPE:]
    cos2, sin2 = _rope_tables(t_lat)
    w1 = {
        "win": jnp.concatenate([w_in, _swap_halves(w_in[:, kr0:])], axis=1).astype(bf16),
        "win_kv": jnp.pad(w_in[:, kv0:], ((0, 0), (0, C_ROPE))).astype(bf16),
        "qng": row(mla_q_norm_g[0]), "kvng": row(mla_kv_norm_g[0]),
        "wkn": w_ukv[:, :, :C_NOPE].reshape(C_KV_LORA, -1).astype(bf16),
        "wv": w_ukv[:, :, C_NOPE:].reshape(C_KV_LORA, -1).astype(bf16),
        "wqn": w_uq[:, :, :C_NOPE].reshape(C_Q_LORA, -1).astype(bf16),
        "wqr": w_qr.reshape(C_Q_LORA, -1).astype(bf16),
        "wqrs": _swap_halves(w_qr).reshape(C_Q_LORA, -1).astype(bf16),
        "cos2": cos2, "sin2": sin2,
    }
    kn_c, v_c, kr_c = _mla_proj(cx, mods[1], row(norm1_g[1]), w1, False, tm_ctx)
    kn_l, v_l, kr_l, qn, qr = _mla_proj(lat, mods[1], row(norm1_g[1]), w1, True, tm_lat)
    att = _attention(qn, qr, kn_c, kr_c, v_c, kn_l, kr_l, v_l, 512)
    lat = _proj_res(lat, att, mla_w_out[0].astype(bf16), mods[1], tm_lat)
    wf = _ffn_weights(ffn_w_up[1], ffn_conv_w[1], ffn_w_down[1])
    return _ffn(lat, mods[1], row(norm2_g[1]), wf, row(final_g), False, tm_lat, True)
```

```python
import functools
import math

import jax
import jax.numpy as jnp
from jax import lax
from jax.experimental import pallas as pl
from jax.experimental.pallas import tpu as pltpu

f32 = jnp.float32
bf16 = jnp.bfloat16

D_MODEL = 1024
GRID_W = 64
A_GROUPS = 4
A_GROUP_DIM = 128
A_WIDTH = A_GROUPS * A_GROUP_DIM
A_CHUNK = 128
B_HEADS = 4
B_HEAD_DIM = 128
B_WIDTH = B_HEADS * B_HEAD_DIM
C_HEADS = 8
C_NOPE = 128
C_ROPE = 64
C_VDIM = 128
C_Q_LORA = 384
C_KV_LORA = 256
ROPE_THETA = 10000.0
D_FF = 2816
EPS = 1e-6

HALO = 8
GDN_CHUNK = 128
FFN_CW = 256
VMEM_LIMIT = 56 * 1024 * 1024

NT = (((1,), (1,)), ((), ()))
TN = (((0,), (0,)), ((), ()))


def _dot(a, b):
    return jnp.dot(a, b, preferred_element_type=f32)


def _silu(x):
    return x * jax.nn.sigmoid(x)


def _gelu(x):
    return 0.5 * x * (1.0 + lax.erf(x * (0.5 ** 0.5)))


def _softplus(x):
    return jnp.maximum(x, 0.0) + jnp.log1p(jnp.exp(-jnp.abs(x)))


def _rms(x, g):
    return x * lax.rsqrt(jnp.mean(x * x, axis=-1, keepdims=True) + EPS) * g


def _norm_mod(x, g, shift, scale):
    return _rms(x, g) * (1.0 + scale) + shift


def _split3(x):
    hi = x.astype(bf16)
    r = x - hi.astype(f32)
    mid = r.astype(bf16)
    lo = (r - mid.astype(f32)).astype(bf16)
    return hi, mid, lo


def _params(n_axes):
    return pltpu.CompilerParams(dimension_semantics=("arbitrary",) * n_axes,
                                vmem_limit_bytes=VMEM_LIMIT)


def _full(shape):
    nd = len(shape)
    return pl.BlockSpec(shape, lambda *_: (0,) * nd, pipeline_mode=pl.Buffered(1))


def _mod_spec(chunk, is_ctx, n_batch):
    if is_ctx:
        return pl.BlockSpec((None, None, 1, D_MODEL), lambda b, i: (n_batch, chunk, 0, 0))
    return pl.BlockSpec((None, None, 1, D_MODEL), lambda b, i: (b, chunk, 0, 0))


def _tok_spec(tm, width):
    return pl.BlockSpec((None, tm, width), lambda b, i: (b, i, 0))


def _halo_specs(tm, t_len):
    r = tm // HALO
    last = t_len // HALO - 1
    return [
        pl.BlockSpec((None, tm, D_MODEL), lambda b, i: (b, i, 0)),
        pl.BlockSpec((None, HALO, D_MODEL), lambda b, i: (b, jnp.maximum(i * r - 1, 0), 0)),
        pl.BlockSpec((None, HALO, D_MODEL), lambda b, i: (b, jnp.minimum((i + 1) * r, last), 0)),
    ]


def _fill_halo_h(x_ref, xp_ref, xn_ref, g_ref, sh_ref, sc_ref, hs_ref, tm):
    i = pl.program_id(1)
    nt = pl.num_programs(1)
    g, sh, sc = g_ref[...], sh_ref[...], sc_ref[...]
    hs_ref[pl.ds(HALO, tm), :] = _norm_mod(x_ref[...], g, sh, sc)
    hs_ref[pl.ds(0, HALO), :] = jnp.where(i > 0, _norm_mod(xp_ref[...], g, sh, sc), 0.0)
    hs_ref[pl.ds(HALO + tm, HALO), :] = jnp.where(i < nt - 1, _norm_mod(xn_ref[...], g, sh, sc), 0.0)


def _conv3(z_ref, cw, tm):
    return (cw[0:1] * z_ref[pl.ds(HALO - 1, tm), :] + cw[1:2] * z_ref[pl.ds(HALO, tm), :]
            + cw[2:3] * z_ref[pl.ds(HALO + 1, tm), :])


def _ada_kernel(c_ref, w_ref, b_ref, o_ref):
    s = _silu(c_ref[...]).astype(bf16)
    o_ref[...] = _dot(s, w_ref[...].astype(bf16)) + b_ref[...]


def _ada_mods(cc, ada_w, ada_b):
    depth = ada_w.shape[0]
    rows = cc.shape[0]
    return pl.pallas_call(
        _ada_kernel,
        out_shape=jax.ShapeDtypeStruct((depth, rows, 6 * D_MODEL), f32),
        grid=(depth, 6),
        in_specs=[pl.BlockSpec((rows, D_MODEL), lambda l, j: (0, 0)),
                  pl.BlockSpec((None, D_MODEL, D_MODEL), lambda l, j: (l, 0, j)),
                  pl.BlockSpec((None, 1, D_MODEL), lambda l, j: (l, 0, j))],
        out_specs=pl.BlockSpec((None, rows, D_MODEL), lambda l, j: (l, 0, j)),
        compiler_params=_params(2),
        name="ada_mods",
    )(cc, ada_w, ada_b.reshape(depth, 1, 6 * D_MODEL))


def _inproj0_kernel(x_ref, xp_ref, xn_ref, g_ref, sh_ref, sc_ref, wqkv_ref, wrest_ref, wab_ref,
                    wabt_ref, conv_ref, auav_ref, gate_ref, q_ref, k_ref, v_ref, ab_ref, abt_ref,
                    hs_ref, zs_ref, *, tm):
    _fill_halo_h(x_ref, xp_ref, xn_ref, g_ref, sh_ref, sc_ref, hs_ref, tm)
    hb_all = hs_ref[...].astype(bf16)
    hb = hs_ref[pl.ds(HALO, tm), :].astype(bf16)
    cols = lambda j: slice(j * B_WIDTH, (j + 1) * B_WIDTH)

    def proj(j):
        zs_ref[j % 2] = _dot(hb_all, wqkv_ref[:, cols(j)])

    def finish(j, o_ref):
        y = _silu(_conv3(zs_ref.at[j % 2], conv_ref[:, cols(j)], tm))
        if j == 2:
            o_ref[...] = y.astype(bf16)
            return
        post = B_HEAD_DIM ** -0.5 if j == 0 else 1.0
        for h in range(B_HEADS):
            yh = y[:, h * B_HEAD_DIM:(h + 1) * B_HEAD_DIM]
            inv = lax.rsqrt(jnp.sum(yh * yh, axis=-1, keepdims=True) + EPS) * post
            o_ref[:, h * B_HEAD_DIM:(h + 1) * B_HEAD_DIM] = (yh * inv).astype(bf16)

    proj(0)
    for j, o_ref in enumerate((q_ref, k_ref, v_ref)):
        if j < 2:
            proj(j + 1)
        finish(j, o_ref)
    auav_ref[:, 0:A_WIDTH] = _dot(hb, wrest_ref[:, 0:A_WIDTH]).astype(bf16)
    auav_ref[:, A_WIDTH:2 * A_WIDTH] = _dot(hb, wrest_ref[:, A_WIDTH:2 * A_WIDTH]).astype(bf16)
    gate_ref[...] = _dot(hb, wrest_ref[:, 2 * A_WIDTH:]).astype(bf16)
    ab_ref[...] = _dot(hb, wab_ref[...])[:, 0:4 * B_HEADS]
    abt_ref[...] = lax.dot_general(wabt_ref[...], hb, NT, preferred_element_type=f32)


def _inproj0(x, mods, norm_g, w, is_ctx, tm):
    n_batch, t_len, _ = x.shape
    nb = mods.shape[0] - 1
    tok = lambda width, dt: jax.ShapeDtypeStruct((n_batch, t_len, width), dt)
    return pl.pallas_call(
        functools.partial(_inproj0_kernel, tm=tm),
        out_shape=(tok(2 * A_WIDTH, bf16), tok(B_WIDTH, bf16), tok(B_WIDTH, bf16), tok(B_WIDTH, bf16),
                   tok(B_WIDTH, bf16), tok(4 * B_HEADS, f32),
                   jax.ShapeDtypeStruct((n_batch, 4 * B_HEADS, t_len), f32)),
        grid=(n_batch, t_len // tm),
        in_specs=_halo_specs(tm, t_len) + [
            _full((1, D_MODEL)), _mod_spec(0, is_ctx, nb), _mod_spec(1, is_ctx, nb),
            _full(w["wqkv"].shape), _full(w["wrest"].shape), _full(w["wab"].shape),
            _full(w["wabt"].shape), _full(w["conv"].shape)],
        out_specs=(_tok_spec(tm, 2 * A_WIDTH), _tok_spec(tm, B_WIDTH), _tok_spec(tm, B_WIDTH),
                   _tok_spec(tm, B_WIDTH), _tok_spec(tm, B_WIDTH), _tok_spec(tm, 4 * B_HEADS),
                   pl.BlockSpec((None, 4 * B_HEADS, tm), lambda b, i: (b, 0, i))),
        scratch_shapes=[pltpu.VMEM((tm + 2 * HALO, D_MODEL), f32),
                        pltpu.VMEM((2, tm + 2 * HALO, B_WIDTH), f32)],
        compiler_params=_params(2),
        name="inproj0_ctx" if is_ctx else "inproj0_lat",
    )(x, x, x, norm_g, mods, mods, w["wqkv"], w["wrest"], w["wab"], w["wabt"], w["conv"])


def _gdn_kernel(qc_ref, kc_ref, vc_ref, gc_ref, abc_ref, abtc_ref,
                ql_ref, kl_ref, vl_ref, gl_ref, abl_ref, abtl_ref,
                alc_ref, dtc_ref, alr_ref, dtr_ref, ng_ref,
                yc_ref, yl_ref,
                q_s, k_s, v_s, o_s, st_s, gcol_s, bcol_s, gtcol_s, grow_s, gtrow_s,
                *, t_ctx, t_lat):
    C = GDN_CHUNK
    n_ctx, n_lat = t_ctx // C, t_lat // C
    n_chunks = n_ctx + n_lat
    nh = B_HEADS

    q_s[pl.ds(0, t_ctx), :] = qc_ref[...]
    q_s[pl.ds(t_ctx, t_lat), :] = ql_ref[...]
    k_s[pl.ds(0, t_ctx), :] = kc_ref[...]
    k_s[pl.ds(t_ctx, t_lat), :] = kl_ref[...]
    v_s[pl.ds(0, t_ctx), :] = vc_ref[...]
    v_s[pl.ds(t_ctx, t_lat), :] = vl_ref[...]
    o_s[...] = jnp.zeros_like(o_s)
    st_s[...] = jnp.zeros_like(st_s)

    ri = lax.broadcasted_iota(jnp.int32, (C, C), 0)
    ci = lax.broadcasted_iota(jnp.int32, (C, C), 1)
    lower = (ri >= ci).astype(bf16)
    upper = (ri <= ci).astype(bf16)
    ones = jnp.ones((C, C), bf16)
    eye = (ri == ci).astype(f32)
    incl = (ri >= ci, ri <= ci)
    strict = (ri > ci, ri < ci)
    pair = ([], [])
    blk = 1
    while blk < C:
        same = (ri // (2 * blk)) == (ci // (2 * blk))
        hi_r, hi_c = (ri % (2 * blk)) >= blk, (ci % (2 * blk)) >= blk
        pair[0].append(same & hi_r & jnp.logical_not(hi_c))
        pair[1].append(same & hi_c & jnp.logical_not(hi_r))
        blk *= 2

    col_is_fwd = lax.broadcasted_iota(jnp.int32, (C, 4 * nh), 1) < nh
    row_is_fwd = lax.broadcasted_iota(jnp.int32, (2 * nh, C), 0) < nh
    for t in range(n_chunks):
        if t < n_ctx:
            ab = abc_ref[pl.ds(t * C, C), :]
            abt = abtc_ref[:, t * C:(t + 1) * C]
        else:
            ab = abl_ref[pl.ds((t - n_ctx) * C, C), :]
            abt = abtl_ref[:, (t - n_ctx) * C:(t - n_ctx + 1) * C]
        la = -jnp.exp(alc_ref[...]) * _softplus(ab + dtc_ref[...])
        p3 = _split3(la)
        pre = sum(_dot(lower, p) for p in p3)
        suf = sum(_dot(upper, p) for p in p3)
        gcol_s[t] = jnp.where(col_is_fwd, pre, suf)
        gtcol_s[t] = sum(_dot(ones, p) for p in p3)
        bcol_s[t] = jax.nn.sigmoid(ab)
        lar = -jnp.exp(alr_ref[...]) * _softplus(abt[0:2 * nh, :] + dtr_ref[...])
        r3 = _split3(lar)
        pre_r = sum(_dot(p, upper) for p in r3)
        suf_r = sum(_dot(p, lower) for p in r3)
        grow_s[t] = jnp.where(row_is_fwd, pre_r, suf_r)
        gtrow_s[t] = sum(_dot(p, ones) for p in r3)

    eye16 = eye.astype(bf16)

    def chunk_load(t, d, h):
        col = d * nh + h
        r0 = pl.multiple_of(t * C, C)
        hs = slice(h * B_HEAD_DIM, (h + 1) * B_HEAD_DIM)
        return dict(
            q=q_s[pl.ds(r0, C), hs], k=k_s[pl.ds(r0, C), hs], v=v_s[pl.ds(r0, C), hs],
            gc=gcol_s[t][:, col:col + 1], gtc=gtcol_s[t][:, col:col + 1],
            bc=bcol_s[t][:, 2 * nh + col:2 * nh + col + 1],
            gr=grow_s[t][col:col + 1, :], gtr=gtrow_s[t][col:col + 1, :],
            s=st_s[col], o=o_s[pl.ds(r0, C), hs])

    def chunk_compute(x, d):
        q, k, v, gc, gtc, bc, gr, gtr, s = (x[n] for n in ("q", "k", "v", "gc", "gtc", "bc", "gr", "gtr", "s"))
        qk_kk = lax.dot_general(jnp.concatenate([q, k], axis=0), k, NT, preferred_element_type=f32)
        k_t = lax.dot_general(eye16, k, NT, preferred_element_type=f32)
        yield
        qk, kk = qk_kk[0:C], qk_kk[C:2 * C]
        decay = jnp.where(incl[d], jnp.exp(jnp.where(incl[d], gc - gr, 0.0)), 0.0)
        a = jnp.where(strict[d], bc * kk * decay, 0.0)
        tinv = eye - jnp.where(pair[d][0], a, 0.0)
        for lvl in range(1, len(pair[d])):
            a_off = jnp.where(pair[d][lvl], a, 0.0).astype(bf16)
            tb16 = tinv.astype(bf16)
            ta = _dot(tb16, a_off).astype(bf16)
            yield
            tinv = tinv - _dot(ta, tb16)
            yield
        e_gc = jnp.exp(gc)
        qf, kf, vf = q.astype(f32), k.astype(f32), v.astype(f32)
        vb_kbg = jnp.concatenate([vf * bc, kf * (bc * e_gc)], axis=1).astype(bf16)
        uw = _dot(tinv.astype(bf16), vb_kbg)
        yield
        u, w = uw[:, 0:B_HEAD_DIM], uw[:, B_HEAD_DIM:]
        attn = (qk * decay).astype(bf16)
        wq_s = _dot(jnp.concatenate([w, qf * e_gc], axis=0).astype(bf16), s.astype(bf16))
        yield
        v_new = (u - wq_s[0:C]).astype(bf16)
        o_new = x["o"] + wq_s[C:2 * C] + _dot(attn, v_new)
        kdec_t = (k_t * jnp.exp(gtr - gr)).astype(bf16)
        s_new = s * jnp.exp(gtc[0:1, :]) + _dot(kdec_t, v_new)
        yield s_new, o_new

    def body(i, carry):
        tb = jnp.where(i < n_ctx, n_ctx - 1 - i, n_chunks + n_ctx - 1 - i)
        systems = [(t, d, h) for h in range(nh) for d, t in ((0, i), (1, tb))]
        gens = [chunk_compute(chunk_load(t, d, h), d) for t, d, h in systems]
        while True:
            stage = [next(g) for g in gens]
            if stage[0] is not None:
                break
        for (t, d, h), (s_new, o_new) in zip(systems, stage):
            st_s[d * nh + h] = s_new
            o_s[pl.ds(pl.multiple_of(t * C, C), C), h * B_HEAD_DIM:(h + 1) * B_HEAD_DIM] = o_new
        return carry

    lax.fori_loop(0, n_chunks, body, 0)

    ng = ng_ref[...]
    for t in range(n_chunks):
        if t < n_ctx:
            gate, y_ref, rows = gc_ref[pl.ds(t * C, C), :], yc_ref, pl.ds(t * C, C)
        else:
            gate, y_ref, rows = gl_ref[pl.ds((t - n_ctx) * C, C), :], yl_ref, pl.ds((t - n_ctx) * C, C)
        o = o_s[pl.ds(t * C, C), :]
        gf = gate.astype(f32)
        for h in range(nh):
            hs = slice(h * B_HEAD_DIM, (h + 1) * B_HEAD_DIM)
            y_ref[rows, hs] = (_rms(o[:, hs], ng) * _silu(gf[:, hs])).astype(bf16)


def _gdn(ctx_parts, lat_parts, a_log, dt_bias, norm_g):
    gc, qc, kc, vc, abc, abtc = ctx_parts
    gl, ql, kl, vl, abl, abtl = lat_parts
    n_batch, t_ctx, _ = qc.shape
    t_lat = ql.shape[1]
    t_all = t_ctx + t_lat
    n_chunks = t_all // GDN_CHUNK
    nh = B_HEADS
    pad = jnp.zeros((2 * nh,), f32)
    alc = jnp.concatenate([a_log.reshape(-1), pad]).reshape(1, 4 * nh)
    dtc = jnp.concatenate([dt_bias.reshape(-1), pad]).reshape(1, 4 * nh)
    alr = jnp.broadcast_to(a_log.reshape(2 * nh, 1), (2 * nh, GDN_CHUNK))
    dtr = jnp.broadcast_to(dt_bias.reshape(2 * nh, 1), (2 * nh, GDN_CHUNK))
    seg = lambda t, wd: pl.BlockSpec((None, t, wd), lambda b: (b, 0, 0))
    segt = lambda t: pl.BlockSpec((None, 4 * nh, t), lambda b: (b, 0, 0))
    one = lambda shape: pl.BlockSpec(shape, lambda b: (0,) * len(shape))
    in_specs = ([seg(t_ctx, B_WIDTH)] * 4 + [seg(t_ctx, 4 * nh), segt(t_ctx)]
                + [seg(t_lat, B_WIDTH)] * 4 + [seg(t_lat, 4 * nh), segt(t_lat)]
                + [one((1, 4 * nh)), one((1, 4 * nh)), one((2 * nh, GDN_CHUNK)),
                   one((2 * nh, GDN_CHUNK)), one((1, B_HEAD_DIM))])
    return pl.pallas_call(
        functools.partial(_gdn_kernel, t_ctx=t_ctx, t_lat=t_lat),
        out_shape=(jax.ShapeDtypeStruct((n_batch, t_ctx, B_WIDTH), bf16),
                   jax.ShapeDtypeStruct((n_batch, t_lat, B_WIDTH), bf16)),
        grid=(n_batch,),
        in_specs=in_specs,
        out_specs=(seg(t_ctx, B_WIDTH), seg(t_lat, B_WIDTH)),
        scratch_shapes=[
            pltpu.VMEM((t_all, B_WIDTH), bf16), pltpu.VMEM((t_all, B_WIDTH), bf16),
            pltpu.VMEM((t_all, B_WIDTH), bf16), pltpu.VMEM((t_all, B_WIDTH), f32),
            pltpu.VMEM((2 * nh, B_HEAD_DIM, B_HEAD_DIM), f32),
            pltpu.VMEM((n_chunks, GDN_CHUNK, 4 * nh), f32), pltpu.VMEM((n_chunks, GDN_CHUNK, 4 * nh), f32),
            pltpu.VMEM((n_chunks, GDN_CHUNK, 4 * nh), f32),
            pltpu.VMEM((n_chunks, 2 * nh, GDN_CHUNK), f32), pltpu.VMEM((n_chunks, 2 * nh, GDN_CHUNK), f32)],
        compiler_params=_params(1),
        name="gdn",
    )(qc, kc, vc, gc, abc, abtc, ql, kl, vl, gl, abl, abtl, alc, dtc, alr, dtr, norm_g)


def _mix0_kernel(x_ref, auav_ref, yb_ref, lng_ref, lnb_ref, ws_ref, bst_ref, wout_ref, g1_ref,
                 o_ref, mix_ref, *, tm):
    au = auav_ref[:, 0:A_WIDTH].astype(f32)
    av = auav_ref[:, A_WIDTH:2 * A_WIDTH].astype(f32)
    u = _gelu(au)
    gv = _gelu(av)
    mu = jnp.mean(gv, axis=-1, keepdims=True)
    dv = gv - mu
    var = jnp.mean(dv * dv, axis=-1, keepdims=True)
    vn = (dv * lax.rsqrt(var + EPS) * lng_ref[...] + lnb_ref[...]).astype(bf16)
    for c in range(tm // A_CHUNK):
        rows = slice(c * A_CHUNK, (c + 1) * A_CHUNK)
        for g in range(A_GROUPS):
            cols = slice(g * A_GROUP_DIM, (g + 1) * A_GROUP_DIM)
            s = _dot(ws_ref[g], vn[rows, cols]) + bst_ref[:, g:g + 1]
            mix_ref[rows, cols] = (u[rows, cols] * s).astype(bf16)
    mix_ref[:, A_WIDTH:] = yb_ref[...]
    o_ref[...] = x_ref[...] + g1_ref[...] * _dot(mix_ref[...], wout_ref[...])


def _mix0(x, auav, yb, mods, w, is_ctx, tm):
    n_batch, t_len, _ = x.shape
    nb = mods.shape[0] - 1
    return pl.pallas_call(
        functools.partial(_mix0_kernel, tm=tm),
        out_shape=jax.ShapeDtypeStruct(x.shape, f32),
        grid=(n_batch, t_len // tm),
        in_specs=[_tok_spec(tm, D_MODEL), _tok_spec(tm, 2 * A_WIDTH), _tok_spec(tm, B_WIDTH),
                  _full((1, A_WIDTH)), _full((1, A_WIDTH)), _full(w["ws"].shape), _full(w["bst"].shape),
                  _full(w["wout"].shape), _mod_spec(2, is_ctx, nb)],
        out_specs=_tok_spec(tm, D_MODEL),
        scratch_shapes=[pltpu.VMEM((tm, A_WIDTH + B_WIDTH), bf16)],
        compiler_params=_params(2),
        name="mix0_ctx" if is_ctx else "mix0_lat",
    )(x, auav, yb, w["lng"], w["lnb"], w["ws"], w["bst"], w["wout"], mods)


def _ffn_kernel(x_ref, xp_ref, xn_ref, g_ref, sh_ref, sc_ref, gt_ref, wup_ref, conv_ref, wdn_ref, fg_ref,
                o_ref, hs_ref, zs_ref, act_ref, *, tm, final_norm):
    _fill_halo_h(x_ref, xp_ref, xn_ref, g_ref, sh_ref, sc_ref, hs_ref, tm)
    hb_all = hs_ref[...].astype(bf16)
    n = D_FF // FFN_CW
    gcols = lambda j: slice(j * FFN_CW, (j + 1) * FFN_CW)
    ucols = lambda j: slice(D_FF + j * FFN_CW, D_FF + (j + 1) * FFN_CW)

    def up(j):
        zs_ref[j % 2, 0] = _dot(hb_all, wup_ref[:, gcols(j)])
        zs_ref[j % 2, 1] = _dot(hb_all, wup_ref[:, ucols(j)])

    def act(j):
        g = _conv3(zs_ref.at[j % 2, 0], conv_ref[:, gcols(j)], tm)
        u = _conv3(zs_ref.at[j % 2, 1], conv_ref[:, ucols(j)], tm)
        act_ref[:, gcols(j)] = (_silu(g) * u).astype(bf16)

    up(0)
    for j in range(n):
        if j + 1 < n:
            up(j + 1)
        act(j)
    out = x_ref[...] + gt_ref[...] * _dot(act_ref[...], wdn_ref[...])
    if final_norm:
        out = _rms(out, fg_ref[...])
    o_ref[...] = out


def _ffn(x, mods, norm_g, w, final_g, is_ctx, tm, final_norm):
    n_batch, t_len, _ = x.shape
    nb = mods.shape[0] - 1
    return pl.pallas_call(
        functools.partial(_ffn_kernel, tm=tm, final_norm=final_norm),
        out_shape=jax.ShapeDtypeStruct(x.shape, f32),
        grid=(n_batch, t_len // tm),
        in_specs=_halo_specs(tm, t_len) + [
            _full((1, D_MODEL)), _mod_spec(3, is_ctx, nb), _mod_spec(4, is_ctx, nb), _mod_spec(5, is_ctx, nb),
            _full(w["wup"].shape), _full(w["conv"].shape), _full(w["wdn"].shape), _full((1, D_MODEL))],
        out_specs=_tok_spec(tm, D_MODEL),
        scratch_shapes=[pltpu.VMEM((tm + 2 * HALO, D_MODEL), f32),
                        pltpu.VMEM((2, 2, tm + 2 * HALO, FFN_CW), f32),
                        pltpu.VMEM((tm, D_FF), bf16)],
        compiler_params=_params(2),
        name="ffn_ctx" if is_ctx else "ffn_lat",
    )(x, x, x, norm_g, mods, mods, mods, w["wup"], w["conv"], w["wdn"], final_g)


def _mla_proj_kernel(*refs, with_q):
    if with_q:
        (x_ref, g_ref, sh_ref, sc_ref, win_ref, qng_ref, kvng_ref, wkn_ref, wv_ref,
         wqn_ref, wqr_ref, wqrs_ref, cos_ref, sin_ref,
         kn_ref, v_ref, kr_ref, qn_ref, qr_ref) = refs
    else:
        (x_ref, g_ref, sh_ref, sc_ref, win_ref, kvng_ref, wkn_ref, wv_ref,
         kn_ref, v_ref, kr_ref) = refs
    hb = _norm_mod(x_ref[...], g_ref[...], sh_ref[...], sc_ref[...]).astype(bf16)
    z = _dot(hb, win_ref[...])
    off = C_Q_LORA if with_q else 0
    ckvn = _rms(z[:, off:off + C_KV_LORA], kvng_ref[...]).astype(bf16)
    kn_ref[...] = _dot(ckvn, wkn_ref[...]).astype(bf16)
    v_ref[...] = _dot(ckvn, wv_ref[...]).astype(bf16)
    kr = z[:, off + C_KV_LORA:off + C_KV_LORA + C_ROPE]
    if not with_q:
        kr_ref[...] = kr.astype(bf16)
        return
    krs = z[:, off + C_KV_LORA + C_ROPE:off + C_KV_LORA + 2 * C_ROPE]
    cos2, sin2 = cos_ref[...], sin_ref[...]
    kr_ref[...] = (kr * cos2[:, 0:C_ROPE] + krs * sin2[:, 0:C_ROPE]).astype(bf16)
    cqn = _rms(z[:, 0:C_Q_LORA], qng_ref[...]).astype(bf16)
    qn_ref[...] = _dot(cqn, wqn_ref[...]).astype(bf16)
    qr = _dot(cqn, wqr_ref[...])
    qrs = _dot(cqn, wqrs_ref[...])
    for hp in range(C_HEADS // 2):
        cols = slice(hp * 2 * C_ROPE, (hp + 1) * 2 * C_ROPE)
        rot = (qr[:, cols] * cos2 + qrs[:, cols] * sin2).astype(bf16)
        qr_ref[2 * hp] = rot[:, 0:C_ROPE]
        qr_ref[2 * hp + 1] = rot[:, C_ROPE:]


def _mla_proj(x, mods, norm_g, w, with_q, tm):
    n_batch, t_len, _ = x.shape
    nb = mods.shape[0] - 1
    is_ctx = not with_q
    hw = C_HEADS * C_NOPE
    tok = lambda width: jax.ShapeDtypeStruct((n_batch, t_len, width), bf16)
    out_shape = [tok(hw), tok(C_HEADS * C_VDIM), tok(C_ROPE)]
    out_specs = [_tok_spec(tm, hw), _tok_spec(tm, C_HEADS * C_VDIM), _tok_spec(tm, C_ROPE)]
    in_specs = [_tok_spec(tm, D_MODEL), _full((1, D_MODEL)), _mod_spec(0, is_ctx, nb), _mod_spec(1, is_ctx, nb)]
    if with_q:
        args = [w["win"], w["qng"], w["kvng"], w["wkn"], w["wv"], w["wqn"], w["wqr"], w["wqrs"]]
        in_specs += [_full(a.shape) for a in args]
        in_specs += [pl.BlockSpec((tm, 2 * C_ROPE), lambda b, i: (i, 0))] * 2
        args += [w["cos2"], w["sin2"]]
        out_shape += [tok(hw), jax.ShapeDtypeStruct((n_batch, C_HEADS, t_len, C_ROPE), bf16)]
        out_specs += [_tok_spec(tm, hw), pl.BlockSpec((None, C_HEADS, tm, C_ROPE), lambda b, i: (b, 0, i, 0))]
    else:
        args = [w["win_kv"], w["kvng"], w["wkn"], w["wv"]]
        in_specs += [_full(a.shape) for a in args]
    return pl.pallas_call(
        functools.partial(_mla_proj_kernel, with_q=with_q),
        out_shape=tuple(out_shape),
        grid=(n_batch, t_len // tm),
        in_specs=in_specs,
        out_specs=tuple(out_specs),
        compiler_params=_params(2),
        name="mla_proj_lat" if with_q else "mla_proj_ctx",
    )(x, norm_g, mods, mods, *args)


def _attn_kernel(qn_ref, qr_ref, knc_ref, krc_ref, vc_ref, knl_ref, krl_ref, vl_ref, o_ref,
                 kc_s, kl_s, q_s, *, tq):
    pad = C_NOPE - C_ROPE
    for dst, n_ref, r_ref in ((kc_s, knc_ref, krc_ref), (kl_s, knl_ref, krl_ref), (q_s, qn_ref, qr_ref)):
        dst[:, 0:C_NOPE] = n_ref[...]
        dst[:, C_NOPE:C_NOPE + C_ROPE] = r_ref[...]
        dst[:, C_NOPE + C_ROPE:] = jnp.zeros((dst.shape[0], pad), bf16)
    n_tiles = q_s.shape[0] // tq

    def scores(i):
        q = q_s[pl.ds(i * tq, tq), :]
        return (lax.dot_general(q, kc_s[...], NT, preferred_element_type=f32),
                lax.dot_general(q, kl_s[...], NT, preferred_element_type=f32))

    def finish(i, sc, sl):
        m = jnp.maximum(jnp.max(sc, axis=-1, keepdims=True), jnp.max(sl, axis=-1, keepdims=True))
        pc = jnp.exp2(sc - m)
        pl_ = jnp.exp2(sl - m)
        denom = jnp.sum(pc, axis=-1, keepdims=True) + jnp.sum(pl_, axis=-1, keepdims=True)
        o = _dot(pc.astype(bf16), vc_ref[...]) + _dot(pl_.astype(bf16), vl_ref[...])
        o_ref[pl.ds(i * tq, tq), :] = (o / denom).astype(bf16)

    nxt = scores(0)
    for i in range(n_tiles):
        cur = nxt
        if i + 1 < n_tiles:
            nxt = scores(i + 1)
        finish(i, *cur)


def _attention(qn, qr, kn_c, kr_c, v_c, kn_l, kr_l, v_l, tq):
    n_batch, t_lat, _ = qn.shape
    t_ctx = kn_c.shape[1]
    head = lambda t: pl.BlockSpec((None, t, C_NOPE), lambda b, h: (b, 0, h))
    rope = lambda t: pl.BlockSpec((None, t, C_ROPE), lambda b, h: (b, 0, 0))
    return pl.pallas_call(
        functools.partial(_attn_kernel, tq=tq),
        out_shape=jax.ShapeDtypeStruct((n_batch, t_lat, C_HEADS * C_VDIM), bf16),
        grid=(n_batch, C_HEADS),
        in_specs=[head(t_lat), pl.BlockSpec((None, None, t_lat, C_ROPE), lambda b, h: (b, h, 0, 0)),
                  head(t_ctx), rope(t_ctx), head(t_ctx), head(t_lat), rope(t_lat), head(t_lat)],
        out_specs=head(t_lat),
        scratch_shapes=[pltpu.VMEM((t_ctx, 2 * C_NOPE), bf16), pltpu.VMEM((t_lat, 2 * C_NOPE), bf16),
                        pltpu.VMEM((t_lat, 2 * C_NOPE), bf16)],
        compiler_params=_params(2),
        name="mla_attention",
    )(qn, qr, kn_c, kr_c, v_c, kn_l, kr_l, v_l)


def _proj_res_kernel(x_ref, y_ref, w_ref, g1_ref, o_ref):
    o_ref[...] = x_ref[...] + g1_ref[...] * _dot(y_ref[...], w_ref[...])


def _proj_res(x, y, w_out, mods, tm):
    n_batch, t_len, _ = x.shape
    nb = mods.shape[0] - 1
    return pl.pallas_call(
        _proj_res_kernel,
        out_shape=jax.ShapeDtypeStruct(x.shape, f32),
        grid=(n_batch, t_len // tm),
        in_specs=[_tok_spec(tm, D_MODEL), _tok_spec(tm, y.shape[-1]), _full(w_out.shape),
                  _mod_spec(2, False, nb)],
        out_specs=_tok_spec(tm, D_MODEL),
        compiler_params=_params(2),
        name="mla_out_proj",
    )(x, y, w_out, mods)


def _rope_tables(n):
    rows = n // GRID_W
    row = jnp.repeat(jnp.arange(rows, dtype=f32), GRID_W)
    col = jnp.tile(jnp.arange(GRID_W, dtype=f32), rows)
    n_freq = C_ROPE // 4
    inv = ROPE_THETA ** (-jnp.arange(n_freq, dtype=f32) / n_freq)
    ang = jnp.concatenate([row[:, None] * inv, col[:, None] * inv], axis=-1)
    cos, sin = jnp.cos(ang), jnp.sin(ang)
    cos64 = jnp.concatenate([cos, cos], axis=-1)
    sin64 = jnp.concatenate([-sin, sin], axis=-1)
    return jnp.tile(cos64, (1, 2)), jnp.tile(sin64, (1, 2))


def _swap_halves(w):
    h = C_ROPE // 2
    return jnp.concatenate([w[..., h:], w[..., :h]], axis=-1)


def _ffn_weights(w_up, conv_w, w_down):
    return {"wup": w_up.astype(bf16), "conv": conv_w, "wdn": w_down.astype(bf16)}


def kernel(x, c, ctx, c_ctx, ada_w, ada_b, norm1_g, norm2_g, ab_w_in, a_ln_g, a_ln_b, a_ws, a_bs, b_conv_w, b_a_log, b_dt_bias, b_norm_g, ab_w_out, mla_w_in, mla_q_norm_g, mla_kv_norm_g, mla_w_uq, mla_w_ukv, mla_w_out, ffn_w_up, ffn_conv_w, ffn_w_down, final_g):
    n_batch, t_lat, d = x.shape
    t_ctx = ctx.shape[1]
    tm_lat, tm_ctx = 512, t_ctx

    cc = jnp.concatenate([c, c_ctx[None, :], jnp.zeros((7, d), f32)], axis=0)
    mods_all = _ada_mods(cc, ada_w, ada_b)
    mods = [mods_all[i, :n_batch + 1].reshape(n_batch + 1, 6, 1, d) for i in range(2)]
    row = lambda v: v.reshape(1, -1)

    a2, b4 = 2 * A_WIDTH, 4 * B_WIDTH
    w_in = ab_w_in[0]
    w_ab = w_in[:, a2 + b4:]
    w0 = {
        "wqkv": w_in[:, a2:a2 + 3 * B_WIDTH].astype(bf16),
        "wrest": jnp.concatenate([w_in[:, :a2], w_in[:, a2 + 3 * B_WIDTH:a2 + b4]], axis=1).astype(bf16),
        "wab": jnp.pad(w_ab, ((0, 0), (0, 128 - 4 * B_HEADS))).astype(bf16),
        "wabt": w_ab.T.astype(bf16),
        "conv": b_conv_w[0],
    }
    parts_c = _inproj0(ctx, mods[0], row(norm1_g[0]), w0, True, tm_ctx)
    parts_l = _inproj0(x, mods[0], row(norm1_g[0]), w0, False, tm_lat)
    yb_c, yb_l = _gdn(parts_c[1:], parts_l[1:], b_a_log[0], b_dt_bias[0], row(b_norm_g[0]))
    wm = {"lng": row(a_ln_g[0]), "lnb": row(a_ln_b[0]), "ws": a_ws[0].astype(bf16), "bst": a_bs[0].T,
          "wout": ab_w_out[0].astype(bf16)}
    cx = _mix0(ctx, parts_c[0], yb_c, mods[0], wm, True, tm_ctx)
    lat = _mix0(x, parts_l[0], yb_l, mods[0], wm, False, tm_lat)
    wf = _ffn_weights(ffn_w_up[0], ffn_conv_w[0], ffn_w_down[0])
    cx = _ffn(cx, mods[0], row(norm2_g[0]), wf, row(final_g), True, tm_ctx, False)
    lat = _ffn(lat, mods[0], row(norm2_g[0]), wf, row(final_g), False, tm_lat, False)

    w_in = mla_w_in[0]
    kv0 = C_Q_LORA
    kr0 = C_Q_LORA + C_KV_LORA
    q_fold = (C_NOPE + C_ROPE) ** -0.5 * math.log2(math.e)
    w_uq = (mla_w_uq[0] * q_fold).reshape(C_Q_LORA, C_HEADS, C_NOPE + C_ROPE)
    w_ukv = mla_w_ukv[0].reshape(C_KV_LORA, C_HEADS, C_NOPE + C_VDIM)
    w_qr = w_uq[:, :, C_NOPE:]
    cos2, sin2 = _rope_tables(t_lat)
    w1 = {
        "win": jnp.concatenate([w_in, _swap_halves(w_in[:, kr0:])], axis=1).astype(bf16),
        "win_kv": jnp.pad(w_in[:, kv0:], ((0, 0), (0, C_ROPE))).astype(bf16),
        "qng": row(mla_q_norm_g[0]), "kvng": row(mla_kv_norm_g[0]),
        "wkn": w_ukv[:, :, :C_NOPE].reshape(C_KV_LORA, -1).astype(bf16),
        "wv": w_ukv[:, :, C_NOPE:].reshape(C_KV_LORA, -1).astype(bf16),
        "wqn": w_uq[:, :, :C_NOPE].reshape(C_Q_LORA, -1).astype(bf16),
        "wqr": w_qr.reshape(C_Q_LORA, -1).astype(bf16),
        "wqrs": _swap_halves(w_qr).reshape(C_Q_LORA, -1).astype(bf16),
        "cos2": cos2, "sin2": sin2,
    }
    kn_c, v_c, kr_c = _mla_proj(cx, mods[1], row(norm1_g[1]), w1, False, tm_ctx)
    kn_l, v_l, kr_l, qn, qr = _mla_proj(lat, mods[1], row(norm1_g[1]), w1, True, tm_lat)
    att = _attention(qn, qr, kn_c, kr_c, v_c, kn_l, kr_l, v_l, 256)
    lat = _proj_res(lat, att, mla_w_out[0].astype(bf16), mods[1], tm_lat)
    wf = _ffn_weights(ffn_w_up[1], ffn_conv_w[1], ffn_w_down[1])
    return _ffn(lat, mods[1], row(norm2_g[1]), wf, row(final_g), False, tm_lat, True)
```

```python
import functools
import math

import jax
import jax.numpy as jnp
from jax import lax
from jax.experimental import pallas as pl
from jax.experimental.pallas import tpu as pltpu

f32 = jnp.float32
bf16 = jnp.bfloat16

D_MODEL = 1024
GRID_W = 64
A_GROUPS = 4
A_GROUP_DIM = 128
A_WIDTH = A_GROUPS * A_GROUP_DIM
A_CHUNK = 128
B_HEADS = 4
B_HEAD_DIM = 128
B_WIDTH = B_HEADS * B_HEAD_DIM
C_HEADS = 8
C_NOPE = 128
C_ROPE = 64
C_VDIM = 128
C_Q_LORA = 384
C_KV_LORA = 256
ROPE_THETA = 10000.0
D_FF = 2816
EPS = 1e-6

HALO = 8
GDN_CHUNK = 128
FFN_CW = 256
VMEM_LIMIT = 56 * 1024 * 1024

NT = (((1,), (1,)), ((), ()))
TN = (((0,), (0,)), ((), ()))


def _dot(a, b):
    return jnp.dot(a, b, preferred_element_type=f32)


def _silu(x):
    return x * jax.nn.sigmoid(x)


def _gelu(x):
    return 0.5 * x * (1.0 + lax.erf(x * (0.5 ** 0.5)))


def _softplus(x):
    return jnp.maximum(x, 0.0) + jnp.log1p(jnp.exp(-jnp.abs(x)))


def _rms(x, g):
    return x * lax.rsqrt(jnp.mean(x * x, axis=-1, keepdims=True) + EPS) * g


def _norm_mod(x, g, shift, scale):
    return _rms(x, g) * (1.0 + scale) + shift


def _split3(x):
    hi = x.astype(bf16)
    r = x - hi.astype(f32)
    mid = r.astype(bf16)
    lo = (r - mid.astype(f32)).astype(bf16)
    return hi, mid, lo


def _params(n_axes):
    return pltpu.CompilerParams(dimension_semantics=("arbitrary",) * n_axes,
                                vmem_limit_bytes=VMEM_LIMIT)


def _full(shape):
    nd = len(shape)
    return pl.BlockSpec(shape, lambda *_: (0,) * nd, pipeline_mode=pl.Buffered(1))


def _mod_spec(chunk, is_ctx, n_batch):
    if is_ctx:
        return pl.BlockSpec((None, None, 1, D_MODEL), lambda b, i: (n_batch, chunk, 0, 0))
    return pl.BlockSpec((None, None, 1, D_MODEL), lambda b, i: (b, chunk, 0, 0))


def _tok_spec(tm, width):
    return pl.BlockSpec((None, tm, width), lambda b, i: (b, i, 0))


def _halo_specs(tm, t_len):
    r = tm // HALO
    last = t_len // HALO - 1
    return [
        pl.BlockSpec((None, tm, D_MODEL), lambda b, i: (b, i, 0)),
        pl.BlockSpec((None, HALO, D_MODEL), lambda b, i: (b, jnp.maximum(i * r - 1, 0), 0)),
        pl.BlockSpec((None, HALO, D_MODEL), lambda b, i: (b, jnp.minimum((i + 1) * r, last), 0)),
    ]


def _fill_halo_h(x_ref, xp_ref, xn_ref, g_ref, sh_ref, sc_ref, hs_ref, tm):
    i = pl.program_id(1)
    nt = pl.num_programs(1)
    g, sh, sc = g_ref[...], sh_ref[...], sc_ref[...]
    hs_ref[pl.ds(HALO, tm), :] = _norm_mod(x_ref[...], g, sh, sc)
    hs_ref[pl.ds(0, HALO), :] = jnp.where(i > 0, _norm_mod(xp_ref[...], g, sh, sc), 0.0)
    hs_ref[pl.ds(HALO + tm, HALO), :] = jnp.where(i < nt - 1, _norm_mod(xn_ref[...], g, sh, sc), 0.0)


def _conv3(z_ref, cw, tm):
    return (cw[0:1] * z_ref[pl.ds(HALO - 1, tm), :] + cw[1:2] * z_ref[pl.ds(HALO, tm), :]
            + cw[2:3] * z_ref[pl.ds(HALO + 1, tm), :])


def _ada_kernel(c_ref, w_ref, b_ref, o_ref):
    s = _silu(c_ref[...]).astype(bf16)
    o_ref[...] = _dot(s, w_ref[...].astype(bf16)) + b_ref[...]


def _ada_mods(cc, ada_w, ada_b):
    depth = ada_w.shape[0]
    rows = cc.shape[0]
    return pl.pallas_call(
        _ada_kernel,
        out_shape=jax.ShapeDtypeStruct((depth, rows, 6 * D_MODEL), f32),
        grid=(depth, 6),
        in_specs=[pl.BlockSpec((rows, D_MODEL), lambda l, j: (0, 0)),
                  pl.BlockSpec((None, D_MODEL, D_MODEL), lambda l, j: (l, 0, j)),
                  pl.BlockSpec((None, 1, D_MODEL), lambda l, j: (l, 0, j))],
        out_specs=pl.BlockSpec((None, rows, D_MODEL), lambda l, j: (l, 0, j)),
        compiler_params=_params(2),
        name="ada_mods",
    )(cc, ada_w, ada_b.reshape(depth, 1, 6 * D_MODEL))


def _inproj0_kernel(x_ref, xp_ref, xn_ref, g_ref, sh_ref, sc_ref, wqkv_ref, wrest_ref, wab_ref,
                    wabt_ref, conv_ref, auav_ref, gate_ref, q_ref, k_ref, v_ref, ab_ref, abt_ref,
                    hs_ref, zs_ref, *, tm):
    _fill_halo_h(x_ref, xp_ref, xn_ref, g_ref, sh_ref, sc_ref, hs_ref, tm)
    hb_all = hs_ref[...].astype(bf16)
    hb = hs_ref[pl.ds(HALO, tm), :].astype(bf16)
    cols = lambda j: slice(j * B_WIDTH, (j + 1) * B_WIDTH)

    def proj(j):
        zs_ref[j % 2] = _dot(hb_all, wqkv_ref[:, cols(j)])

    def finish(j, o_ref):
        y = _silu(_conv3(zs_ref.at[j % 2], conv_ref[:, cols(j)], tm))
        if j == 2:
            o_ref[...] = y.astype(bf16)
            return
        post = B_HEAD_DIM ** -0.5 if j == 0 else 1.0
        for h in range(B_HEADS):
            yh = y[:, h * B_HEAD_DIM:(h + 1) * B_HEAD_DIM]
            inv = lax.rsqrt(jnp.sum(yh * yh, axis=-1, keepdims=True) + EPS) * post
            o_ref[:, h * B_HEAD_DIM:(h + 1) * B_HEAD_DIM] = (yh * inv).astype(bf16)

    proj(0)
    for j, o_ref in enumerate((q_ref, k_ref, v_ref)):
        if j < 2:
            proj(j + 1)
        finish(j, o_ref)
    auav_ref[:, 0:A_WIDTH] = _dot(hb, wrest_ref[:, 0:A_WIDTH]).astype(bf16)
    auav_ref[:, A_WIDTH:2 * A_WIDTH] = _dot(hb, wrest_ref[:, A_WIDTH:2 * A_WIDTH]).astype(bf16)
    gate_ref[...] = _dot(hb, wrest_ref[:, 2 * A_WIDTH:]).astype(bf16)
    ab_ref[...] = _dot(hb, wab_ref[...])[:, 0:4 * B_HEADS]
    abt_ref[...] = lax.dot_general(wabt_ref[...], hb, NT, preferred_element_type=f32)


def _inproj0(x, mods, norm_g, w, is_ctx, tm):
    n_batch, t_len, _ = x.shape
    nb = mods.shape[0] - 1
    tok = lambda width, dt: jax.ShapeDtypeStruct((n_batch, t_len, width), dt)
    return pl.pallas_call(
        functools.partial(_inproj0_kernel, tm=tm),
        out_shape=(tok(2 * A_WIDTH, bf16), tok(B_WIDTH, bf16), tok(B_WIDTH, bf16), tok(B_WIDTH, bf16),
                   tok(B_WIDTH, bf16), tok(4 * B_HEADS, f32),
                   jax.ShapeDtypeStruct((n_batch, 4 * B_HEADS, t_len), f32)),
        grid=(n_batch, t_len // tm),
        in_specs=_halo_specs(tm, t_len) + [
            _full((1, D_MODEL)), _mod_spec(0, is_ctx, nb), _mod_spec(1, is_ctx, nb),
            _full(w["wqkv"].shape), _full(w["wrest"].shape), _full(w["wab"].shape),
            _full(w["wabt"].shape), _full(w["conv"].shape)],
        out_specs=(_tok_spec(tm, 2 * A_WIDTH), _tok_spec(tm, B_WIDTH), _tok_spec(tm, B_WIDTH),
                   _tok_spec(tm, B_WIDTH), _tok_spec(tm, B_WIDTH), _tok_spec(tm, 4 * B_HEADS),
                   pl.BlockSpec((None, 4 * B_HEADS, tm), lambda b, i: (b, 0, i))),
        scratch_shapes=[pltpu.VMEM((tm + 2 * HALO, D_MODEL), f32),
                        pltpu.VMEM((2, tm + 2 * HALO, B_WIDTH), f32)],
        compiler_params=_params(2),
        name="inproj0_ctx" if is_ctx else "inproj0_lat",
    )(x, x, x, norm_g, mods, mods, w["wqkv"], w["wrest"], w["wab"], w["wabt"], w["conv"])


def _gdn_kernel(qc_ref, kc_ref, vc_ref, gc_ref, abc_ref, abtc_ref,
                ql_ref, kl_ref, vl_ref, gl_ref, abl_ref, abtl_ref,
                alc_ref, dtc_ref, alr_ref, dtr_ref, ng_ref,
                yc_ref, yl_ref,
                q_s, k_s, v_s, o_s, st_s, gcol_s, bcol_s, gtcol_s, grow_s, gtrow_s,
                *, t_ctx, t_lat):
    C = GDN_CHUNK
    n_ctx, n_lat = t_ctx // C, t_lat // C
    n_chunks = n_ctx + n_lat
    nh = B_HEADS

    q_s[pl.ds(0, t_ctx), :] = qc_ref[...]
    q_s[pl.ds(t_ctx, t_lat), :] = ql_ref[...]
    k_s[pl.ds(0, t_ctx), :] = kc_ref[...]
    k_s[pl.ds(t_ctx, t_lat), :] = kl_ref[...]
    v_s[pl.ds(0, t_ctx), :] = vc_ref[...]
    v_s[pl.ds(t_ctx, t_lat), :] = vl_ref[...]
    o_s[...] = jnp.zeros_like(o_s)
    st_s[...] = jnp.zeros_like(st_s)

    ri = lax.broadcasted_iota(jnp.int32, (C, C), 0)
    ci = lax.broadcasted_iota(jnp.int32, (C, C), 1)
    lower = (ri >= ci).astype(bf16)
    upper = (ri <= ci).astype(bf16)
    ones = jnp.ones((C, C), bf16)
    eye = (ri == ci).astype(f32)
    incl = (ri >= ci, ri <= ci)
    strict = (ri > ci, ri < ci)
    pair = ([], [])
    blk = 1
    while blk < C:
        same = (ri // (2 * blk)) == (ci // (2 * blk))
        hi_r, hi_c = (ri % (2 * blk)) >= blk, (ci % (2 * blk)) >= blk
        pair[0].append(same & hi_r & jnp.logical_not(hi_c))
        pair[1].append(same & hi_c & jnp.logical_not(hi_r))
        blk *= 2

    col_is_fwd = lax.broadcasted_iota(jnp.int32, (C, 4 * nh), 1) < nh
    row_is_fwd = lax.broadcasted_iota(jnp.int32, (2 * nh, C), 0) < nh
    for t in range(n_chunks):
        if t < n_ctx:
            ab = abc_ref[pl.ds(t * C, C), :]
            abt = abtc_ref[:, t * C:(t + 1) * C]
        else:
            ab = abl_ref[pl.ds((t - n_ctx) * C, C), :]
            abt = abtl_ref[:, (t - n_ctx) * C:(t - n_ctx + 1) * C]
        la = -jnp.exp(alc_ref[...]) * _softplus(ab + dtc_ref[...])
        p3 = _split3(la)
        pre = sum(_dot(lower, p) for p in p3)
        suf = sum(_dot(upper, p) for p in p3)
        gcol_s[t] = jnp.where(col_is_fwd, pre, suf)
        gtcol_s[t] = sum(_dot(ones, p) for p in p3)
        bcol_s[t] = jax.nn.sigmoid(ab)
        lar = -jnp.exp(alr_ref[...]) * _softplus(abt[0:2 * nh, :] + dtr_ref[...])
        r3 = _split3(lar)
        pre_r = sum(_dot(p, upper) for p in r3)
        suf_r = sum(_dot(p, lower) for p in r3)
        grow_s[t] = jnp.where(row_is_fwd, pre_r, suf_r)
        gtrow_s[t] = sum(_dot(p, ones) for p in r3)

    def chunk_load(t, d, h):
        col = d * nh + h
        r0 = pl.multiple_of(t * C, C)
        hs = slice(h * B_HEAD_DIM, (h + 1) * B_HEAD_DIM)
        return dict(
            q=q_s[pl.ds(r0, C), hs], k=k_s[pl.ds(r0, C), hs], v=v_s[pl.ds(r0, C), hs],
            gc=gcol_s[t][:, col:col + 1], gtc=gtcol_s[t][:, col:col + 1],
            bc=bcol_s[t][:, 2 * nh + col:2 * nh + col + 1],
            gr=grow_s[t][col:col + 1, :], gtr=gtrow_s[t][col:col + 1, :])

    def block_rows(s, d):
        return [(2 * m + 1 - d) * s for m in range(C // (2 * s))]

    def chunk_intra(x, d):
        q, k, v, gc, gtc, bc, gr, gtr = (x[n] for n in ("q", "k", "v", "gc", "gtc", "bc", "gr", "gtr"))
        qk_kk = lax.dot_general(jnp.concatenate([q, k], axis=0), k, NT, preferred_element_type=f32)
        yield
        qk, kk = qk_kk[0:C], qk_kk[C:2 * C]
        decay = jnp.where(incl[d], jnp.exp(jnp.where(incl[d], gc - gr, 0.0)), 0.0)
        a = jnp.where(strict[d], bc * kk * decay, 0.0)
        tinv = eye - jnp.where(pair[d][0], a, 0.0)
        for lvl in range(1, len(pair[d])):
            s = 2 ** lvl
            a_off = jnp.where(pair[d][lvl], a, 0.0).astype(bf16)
            tb16 = tinv.astype(bf16)
            if s < 8:
                ta = _dot(tb16, a_off).astype(bf16)
                yield
                tinv = tinv - _dot(ta, tb16)
                yield
                continue
            offs = block_rows(s, d)
            t_sel = jnp.concatenate([tinv[o:o + s] for o in offs], axis=0)
            ta = _dot(t_sel.astype(bf16), a_off).astype(bf16)
            yield
            t_sel = t_sel - _dot(ta, tb16)
            yield
            pieces, at = [], 0
            for m, o in enumerate(offs):
                pieces += [tinv[at:o], t_sel[m * s:(m + 1) * s]]
                at = o + s
            tinv = jnp.concatenate([p for p in pieces + [tinv[at:C]] if p.shape[0]], axis=0)
        e_gc = jnp.exp(gc)
        qf, kf, vf = q.astype(f32), k.astype(f32), v.astype(f32)
        vb_kbg = jnp.concatenate([vf * bc, kf * (bc * e_gc)], axis=1).astype(bf16)
        uw = _dot(tinv.astype(bf16), vb_kbg)
        yield
        yield dict(u=uw[:, 0:B_HEAD_DIM],
                   wq=jnp.concatenate([uw[:, B_HEAD_DIM:], qf * e_gc], axis=0).astype(bf16),
                   attn=(qk * decay).astype(bf16),
                   kdec_t=(kf.T * jnp.exp(gtr - gr)).astype(bf16),
                   gend=jnp.exp(gtc[0:1, :]))

    def chunk_state(p, s):
        wq_s = _dot(p["wq"], s.astype(bf16))
        yield
        v_new = (p["u"] - wq_s[0:C]).astype(bf16)
        yield s * p["gend"] + _dot(p["kdec_t"], v_new), wq_s[C:2 * C] + _dot(p["attn"], v_new)

    def round_robin(gens):
        while True:
            stage = [next(g) for g in gens]
            if stage[0] is not None:
                return stage

    steps = 2

    def body(j, carry):
        systems = []
        for st in range(steps):
            i = j * steps + st
            tb = jnp.where(i < n_ctx, n_ctx - 1 - i, n_chunks + n_ctx - 1 - i)
            systems += [(t, d, h) for h in range(nh) for d, t in ((0, i), (1, tb))]
        prods = round_robin([chunk_intra(chunk_load(t, d, h), d) for t, d, h in systems])
        state = [st_s[c] for c in range(2 * nh)]
        for st in range(steps):
            group = list(zip(systems, prods))[st * 2 * nh:(st + 1) * 2 * nh]
            outs = round_robin([chunk_state(p, state[d * nh + h]) for (_, d, h), p in group])
            for ((t, d, h), _), (s_new, o_add) in zip(group, outs):
                state[d * nh + h] = s_new
                o_s[pl.ds(pl.multiple_of(t * C, C), C), h * B_HEAD_DIM:(h + 1) * B_HEAD_DIM] += o_add
        for c in range(2 * nh):
            st_s[c] = state[c]
        return carry

    assert n_chunks % steps == 0
    lax.fori_loop(0, n_chunks // steps, body, 0)

    ng = ng_ref[...]
    for t in range(n_chunks):
        if t < n_ctx:
            gate, y_ref, rows = gc_ref[pl.ds(t * C, C), :], yc_ref, pl.ds(t * C, C)
        else:
            gate, y_ref, rows = gl_ref[pl.ds((t - n_ctx) * C, C), :], yl_ref, pl.ds((t - n_ctx) * C, C)
        o = o_s[pl.ds(t * C, C), :]
        gf = gate.astype(f32)
        for h in range(nh):
            hs = slice(h * B_HEAD_DIM, (h + 1) * B_HEAD_DIM)
            y_ref[rows, hs] = (_rms(o[:, hs], ng) * _silu(gf[:, hs])).astype(bf16)


def _gdn(ctx_parts, lat_parts, a_log, dt_bias, norm_g):
    gc, qc, kc, vc, abc, abtc = ctx_parts
    gl, ql, kl, vl, abl, abtl = lat_parts
    n_batch, t_ctx, _ = qc.shape
    t_lat = ql.shape[1]
    t_all = t_ctx + t_lat
    n_chunks = t_all // GDN_CHUNK
    nh = B_HEADS
    pad = jnp.zeros((2 * nh,), f32)
    alc = jnp.concatenate([a_log.reshape(-1), pad]).reshape(1, 4 * nh)
    dtc = jnp.concatenate([dt_bias.reshape(-1), pad]).reshape(1, 4 * nh)
    alr = jnp.broadcast_to(a_log.reshape(2 * nh, 1), (2 * nh, GDN_CHUNK))
    dtr = jnp.broadcast_to(dt_bias.reshape(2 * nh, 1), (2 * nh, GDN_CHUNK))
    seg = lambda t, wd: pl.BlockSpec((None, t, wd), lambda b: (b, 0, 0))
    segt = lambda t: pl.BlockSpec((None, 4 * nh, t), lambda b: (b, 0, 0))
    one = lambda shape: pl.BlockSpec(shape, lambda b: (0,) * len(shape))
    in_specs = ([seg(t_ctx, B_WIDTH)] * 4 + [seg(t_ctx, 4 * nh), segt(t_ctx)]
                + [seg(t_lat, B_WIDTH)] * 4 + [seg(t_lat, 4 * nh), segt(t_lat)]
                + [one((1, 4 * nh)), one((1, 4 * nh)), one((2 * nh, GDN_CHUNK)),
                   one((2 * nh, GDN_CHUNK)), one((1, B_HEAD_DIM))])
    return pl.pallas_call(
        functools.partial(_gdn_kernel, t_ctx=t_ctx, t_lat=t_lat),
        out_shape=(jax.ShapeDtypeStruct((n_batch, t_ctx, B_WIDTH), bf16),
                   jax.ShapeDtypeStruct((n_batch, t_lat, B_WIDTH), bf16)),
        grid=(n_batch,),
        in_specs=in_specs,
        out_specs=(seg(t_ctx, B_WIDTH), seg(t_lat, B_WIDTH)),
        scratch_shapes=[
            pltpu.VMEM((t_all, B_WIDTH), bf16), pltpu.VMEM((t_all, B_WIDTH), bf16),
            pltpu.VMEM((t_all, B_WIDTH), bf16), pltpu.VMEM((t_all, B_WIDTH), f32),
            pltpu.VMEM((2 * nh, B_HEAD_DIM, B_HEAD_DIM), f32),
            pltpu.VMEM((n_chunks, GDN_CHUNK, 4 * nh), f32), pltpu.VMEM((n_chunks, GDN_CHUNK, 4 * nh), f32),
            pltpu.VMEM((n_chunks, GDN_CHUNK, 4 * nh), f32),
            pltpu.VMEM((n_chunks, 2 * nh, GDN_CHUNK), f32), pltpu.VMEM((n_chunks, 2 * nh, GDN_CHUNK), f32)],
        compiler_params=_params(1),
        name="gdn",
    )(qc, kc, vc, gc, abc, abtc, ql, kl, vl, gl, abl, abtl, alc, dtc, alr, dtr, norm_g)


def _mix0_kernel(x_ref, auav_ref, yb_ref, lng_ref, lnb_ref, ws_ref, bst_ref, wout_ref, g1_ref,
                 o_ref, mix_ref, *, tm):
    au = auav_ref[:, 0:A_WIDTH].astype(f32)
    av = auav_ref[:, A_WIDTH:2 * A_WIDTH].astype(f32)
    u = _gelu(au)
    gv = _gelu(av)
    mu = jnp.mean(gv, axis=-1, keepdims=True)
    dv = gv - mu
    var = jnp.mean(dv * dv, axis=-1, keepdims=True)
    vn = (dv * lax.rsqrt(var + EPS) * lng_ref[...] + lnb_ref[...]).astype(bf16)
    for c in range(tm // A_CHUNK):
        rows = slice(c * A_CHUNK, (c + 1) * A_CHUNK)
        for g in range(A_GROUPS):
            cols = slice(g * A_GROUP_DIM, (g + 1) * A_GROUP_DIM)
            s = _dot(ws_ref[g], vn[rows, cols]) + bst_ref[:, g:g + 1]
            mix_ref[rows, cols] = (u[rows, cols] * s).astype(bf16)
    mix_ref[:, A_WIDTH:] = yb_ref[...]
    o_ref[...] = x_ref[...] + g1_ref[...] * _dot(mix_ref[...], wout_ref[...])


def _mix0(x, auav, yb, mods, w, is_ctx, tm):
    n_batch, t_len, _ = x.shape
    nb = mods.shape[0] - 1
    return pl.pallas_call(
        functools.partial(_mix0_kernel, tm=tm),
        out_shape=jax.ShapeDtypeStruct(x.shape, f32),
        grid=(n_batch, t_len // tm),
        in_specs=[_tok_spec(tm, D_MODEL), _tok_spec(tm, 2 * A_WIDTH), _tok_spec(tm, B_WIDTH),
                  _full((1, A_WIDTH)), _full((1, A_WIDTH)), _full(w["ws"].shape), _full(w["bst"].shape),
                  _full(w["wout"].shape), _mod_spec(2, is_ctx, nb)],
        out_specs=_tok_spec(tm, D_MODEL),
        scratch_shapes=[pltpu.VMEM((tm, A_WIDTH + B_WIDTH), bf16)],
        compiler_params=_params(2),
        name="mix0_ctx" if is_ctx else "mix0_lat",
    )(x, auav, yb, w["lng"], w["lnb"], w["ws"], w["bst"], w["wout"], mods)


def _ffn_kernel(x_ref, xp_ref, xn_ref, g_ref, sh_ref, sc_ref, gt_ref, wup_ref, conv_ref, wdn_ref, fg_ref,
                o_ref, hs_ref, zs_ref, act_ref, *, tm, final_norm):
    _fill_halo_h(x_ref, xp_ref, xn_ref, g_ref, sh_ref, sc_ref, hs_ref, tm)
    hb_all = hs_ref[...].astype(bf16)
    n = D_FF // FFN_CW
    gcols = lambda j: slice(j * FFN_CW, (j + 1) * FFN_CW)
    ucols = lambda j: slice(D_FF + j * FFN_CW, D_FF + (j + 1) * FFN_CW)

    def up(j):
        zs_ref[j % 2, 0] = _dot(hb_all, wup_ref[:, gcols(j)])
        zs_ref[j % 2, 1] = _dot(hb_all, wup_ref[:, ucols(j)])

    def act(j):
        g = _conv3(zs_ref.at[j % 2, 0], conv_ref[:, gcols(j)], tm)
        u = _conv3(zs_ref.at[j % 2, 1], conv_ref[:, ucols(j)], tm)
        act_ref[:, gcols(j)] = (_silu(g) * u).astype(bf16)

    up(0)
    for j in range(n):
        if j + 1 < n:
            up(j + 1)
        act(j)
    out = x_ref[...] + gt_ref[...] * _dot(act_ref[...], wdn_ref[...])
    if final_norm:
        out = _rms(out, fg_ref[...])
    o_ref[...] = out


def _ffn(x, mods, norm_g, w, final_g, is_ctx, tm, final_norm):
    n_batch, t_len, _ = x.shape
    nb = mods.shape[0] - 1
    return pl.pallas_call(
        functools.partial(_ffn_kernel, tm=tm, final_norm=final_norm),
        out_shape=jax.ShapeDtypeStruct(x.shape, f32),
        grid=(n_batch, t_len // tm),
        in_specs=_halo_specs(tm, t_len) + [
            _full((1, D_MODEL)), _mod_spec(3, is_ctx, nb), _mod_spec(4, is_ctx, nb), _mod_spec(5, is_ctx, nb),
            _full(w["wup"].shape), _full(w["conv"].shape), _full(w["wdn"].shape), _full((1, D_MODEL))],
        out_specs=_tok_spec(tm, D_MODEL),
        scratch_shapes=[pltpu.VMEM((tm + 2 * HALO, D_MODEL), f32),
                        pltpu.VMEM((2, 2, tm + 2 * HALO, FFN_CW), f32),
                        pltpu.VMEM((tm, D_FF), bf16)],
        compiler_params=_params(2),
        name="ffn_ctx" if is_ctx else "ffn_lat",
    )(x, x, x, norm_g, mods, mods, mods, w["wup"], w["conv"], w["wdn"], final_g)


def _mla_proj_kernel(*refs, with_q):
    if with_q:
        (x_ref, g_ref, sh_ref, sc_ref, win_ref, qng_ref, kvng_ref, wkn_ref, wv_ref,
         wqn_ref, wqr_ref, wqrs_ref, cos_ref, sin_ref,
         kn_ref, v_ref, kr_ref, qn_ref, qr_ref) = refs
    else:
        (x_ref, g_ref, sh_ref, sc_ref, win_ref, kvng_ref, wkn_ref, wv_ref,
         kn_ref, v_ref, kr_ref) = refs
    hb = _norm_mod(x_ref[...], g_ref[...], sh_ref[...], sc_ref[...]).astype(bf16)
    z = _dot(hb, win_ref[...])
    off = C_Q_LORA if with_q else 0
    ckvn = _rms(z[:, off:off + C_KV_LORA], kvng_ref[...]).astype(bf16)
    kn_ref[...] = _dot(ckvn, wkn_ref[...]).astype(bf16)
    v_ref[...] = _dot(ckvn, wv_ref[...]).astype(bf16)
    kr = z[:, off + C_KV_LORA:off + C_KV_LORA + C_ROPE]
    if not with_q:
        kr_ref[...] = kr.astype(bf16)
        return
    krs = z[:, off + C_KV_LORA + C_ROPE:off + C_KV_LORA + 2 * C_ROPE]
    cos2, sin2 = cos_ref[...], sin_ref[...]
    kr_ref[...] = (kr * cos2[:, 0:C_ROPE] + krs * sin2[:, 0:C_ROPE]).astype(bf16)
    cqn = _rms(z[:, 0:C_Q_LORA], qng_ref[...]).astype(bf16)
    qn_ref[...] = _dot(cqn, wqn_ref[...]).astype(bf16)
    qr = _dot(cqn, wqr_ref[...])
    qrs = _dot(cqn, wqrs_ref[...])
    for hp in range(C_HEADS // 2):
        cols = slice(hp * 2 * C_ROPE, (hp + 1) * 2 * C_ROPE)
        rot = (qr[:, cols] * cos2 + qrs[:, cols] * sin2).astype(bf16)
        qr_ref[2 * hp] = rot[:, 0:C_ROPE]
        qr_ref[2 * hp + 1] = rot[:, C_ROPE:]


def _mla_proj(x, mods, norm_g, w, with_q, tm):
    n_batch, t_len, _ = x.shape
    nb = mods.shape[0] - 1
    is_ctx = not with_q
    hw = C_HEADS * C_NOPE
    tok = lambda width: jax.ShapeDtypeStruct((n_batch, t_len, width), bf16)
    out_shape = [tok(hw), tok(C_HEADS * C_VDIM), tok(C_ROPE)]
    out_specs = [_tok_spec(tm, hw), _tok_spec(tm, C_HEADS * C_VDIM), _tok_spec(tm, C_ROPE)]
    in_specs = [_tok_spec(tm, D_MODEL), _full((1, D_MODEL)), _mod_spec(0, is_ctx, nb), _mod_spec(1, is_ctx, nb)]
    if with_q:
        args = [w["win"], w["qng"], w["kvng"], w["wkn"], w["wv"], w["wqn"], w["wqr"], w["wqrs"]]
        in_specs += [_full(a.shape) for a in args]
        in_specs += [pl.BlockSpec((tm, 2 * C_ROPE), lambda b, i: (i, 0))] * 2
        args += [w["cos2"], w["sin2"]]
        out_shape += [tok(hw), jax.ShapeDtypeStruct((n_batch, C_HEADS, t_len, C_ROPE), bf16)]
        out_specs += [_tok_spec(tm, hw), pl.BlockSpec((None, C_HEADS, tm, C_ROPE), lambda b, i: (b, 0, i, 0))]
    else:
        args = [w["win_kv"], w["kvng"], w["wkn"], w["wv"]]
        in_specs += [_full(a.shape) for a in args]
    return pl.pallas_call(
        functools.partial(_mla_proj_kernel, with_q=with_q),
        out_shape=tuple(out_shape),
        grid=(n_batch, t_len // tm),
        in_specs=in_specs,
        out_specs=tuple(out_specs),
        compiler_params=_params(2),
        name="mla_proj_lat" if with_q else "mla_proj_ctx",
    )(x, norm_g, mods, mods, *args)


def _attn_kernel(qn_ref, qr_ref, knc_ref, krc_ref, vc_ref, knl_ref, krl_ref, vl_ref, o_ref,
                 kc_s, kl_s, q_s, *, tq):
    pad = C_NOPE - C_ROPE
    for dst, n_ref, r_ref in ((kc_s, knc_ref, krc_ref), (kl_s, knl_ref, krl_ref), (q_s, qn_ref, qr_ref)):
        dst[:, 0:C_NOPE] = n_ref[...]
        dst[:, C_NOPE:C_NOPE + C_ROPE] = r_ref[...]
        dst[:, C_NOPE + C_ROPE:] = jnp.zeros((dst.shape[0], pad), bf16)
    n_tiles = q_s.shape[0] // tq

    def scores(i):
        q = q_s[pl.ds(i * tq, tq), :]
        return (lax.dot_general(q, kc_s[...], NT, preferred_element_type=f32),
                lax.dot_general(q, kl_s[...], NT, preferred_element_type=f32))

    def finish(i, sc, sl):
        m = jnp.maximum(jnp.max(sc, axis=-1, keepdims=True), jnp.max(sl, axis=-1, keepdims=True))
        pc = jnp.exp2(sc - m)
        pl_ = jnp.exp2(sl - m)
        denom = jnp.sum(pc, axis=-1, keepdims=True) + jnp.sum(pl_, axis=-1, keepdims=True)
        o = _dot(pc.astype(bf16), vc_ref[...]) + _dot(pl_.astype(bf16), vl_ref[...])
        o_ref[pl.ds(i * tq, tq), :] = (o / denom).astype(bf16)

    nxt = scores(0)
    for i in range(n_tiles):
        cur = nxt
        if i + 1 < n_tiles:
            nxt = scores(i + 1)
        finish(i, *cur)


def _attention(qn, qr, kn_c, kr_c, v_c, kn_l, kr_l, v_l, tq):
    n_batch, t_lat, _ = qn.shape
    t_ctx = kn_c.shape[1]
    head = lambda t: pl.BlockSpec((None, t, C_NOPE), lambda b, h: (b, 0, h))
    rope = lambda t: pl.BlockSpec((None, t, C_ROPE), lambda b, h: (b, 0, 0))
    return pl.pallas_call(
        functools.partial(_attn_kernel, tq=tq),
        out_shape=jax.ShapeDtypeStruct((n_batch, t_lat, C_HEADS * C_VDIM), bf16),
        grid=(n_batch, C_HEADS),
        in_specs=[head(t_lat), pl.BlockSpec((None, None, t_lat, C_ROPE), lambda b, h: (b, h, 0, 0)),
                  head(t_ctx), rope(t_ctx), head(t_ctx), head(t_lat), rope(t_lat), head(t_lat)],
        out_specs=head(t_lat),
        scratch_shapes=[pltpu.VMEM((t_ctx, 2 * C_NOPE), bf16), pltpu.VMEM((t_lat, 2 * C_NOPE), bf16),
                        pltpu.VMEM((t_lat, 2 * C_NOPE), bf16)],
        compiler_params=_params(2),
        name="mla_attention",
    )(qn, qr, kn_c, kr_c, v_c, kn_l, kr_l, v_l)


def _proj_res_kernel(x_ref, y_ref, w_ref, g1_ref, o_ref):
    o_ref[...] = x_ref[...] + g1_ref[...] * _dot(y_ref[...], w_ref[...])


def _proj_res(x, y, w_out, mods, tm):
    n_batch, t_len, _ = x.shape
    nb = mods.shape[0] - 1
    return pl.pallas_call(
        _proj_res_kernel,
        out_shape=jax.ShapeDtypeStruct(x.shape, f32),
        grid=(n_batch, t_len // tm),
        in_specs=[_tok_spec(tm, D_MODEL), _tok_spec(tm, y.shape[-1]), _full(w_out.shape),
                  _mod_spec(2, False, nb)],
        out_specs=_tok_spec(tm, D_MODEL),
        compiler_params=_params(2),
        name="mla_out_proj",
    )(x, y, w_out, mods)


def _rope_tables(n):
    rows = n // GRID_W
    row = jnp.repeat(jnp.arange(rows, dtype=f32), GRID_W)
    col = jnp.tile(jnp.arange(GRID_W, dtype=f32), rows)
    n_freq = C_ROPE // 4
    inv = ROPE_THETA ** (-jnp.arange(n_freq, dtype=f32) / n_freq)
    ang = jnp.concatenate([row[:, None] * inv, col[:, None] * inv], axis=-1)
    cos, sin = jnp.cos(ang), jnp.sin(ang)
    cos64 = jnp.concatenate([cos, cos], axis=-1)
    sin64 = jnp.concatenate([-sin, sin], axis=-1)
    return jnp.tile(cos64, (1, 2)), jnp.tile(sin64, (1, 2))


def _swap_halves(w):
    h = C_ROPE // 2
    return jnp.concatenate([w[..., h:], w[..., :h]], axis=-1)


def _ffn_weights(w_up, conv_w, w_down):
    return {"wup": w_up.astype(bf16), "conv": conv_w, "wdn": w_down.astype(bf16)}


def kernel(x, c, ctx, c_ctx, ada_w, ada_b, norm1_g, norm2_g, ab_w_in, a_ln_g, a_ln_b, a_ws, a_bs, b_conv_w, b_a_log, b_dt_bias, b_norm_g, ab_w_out, mla_w_in, mla_q_norm_g, mla_kv_norm_g, mla_w_uq, mla_w_ukv, mla_w_out, ffn_w_up, ffn_conv_w, ffn_w_down, final_g):
    n_batch, t_lat, d = x.shape
    t_ctx = ctx.shape[1]
    tm_lat, tm_ctx, tm_ffn = 512, t_ctx, 512

    cc = jnp.concatenate([c, c_ctx[None, :], jnp.zeros((7, d), f32)], axis=0)
    mods_all = _ada_mods(cc, ada_w, ada_b)
    mods = [mods_all[i, :n_batch + 1].reshape(n_batch + 1, 6, 1, d) for i in range(2)]
    row = lambda v: v.reshape(1, -1)

    a2, b4 = 2 * A_WIDTH, 4 * B_WIDTH
    w_in = ab_w_in[0]
    w_ab = w_in[:, a2 + b4:]
    w0 = {
        "wqkv": w_in[:, a2:a2 + 3 * B_WIDTH].astype(bf16),
        "wrest": jnp.concatenate([w_in[:, :a2], w_in[:, a2 + 3 * B_WIDTH:a2 + b4]], axis=1).astype(bf16),
        "wab": jnp.pad(w_ab, ((0, 0), (0, 128 - 4 * B_HEADS))).astype(bf16),
        "wabt": w_ab.T.astype(bf16),
        "conv": b_conv_w[0],
    }
    parts_c = _inproj0(ctx, mods[0], row(norm1_g[0]), w0, True, tm_ctx)
    parts_l = _inproj0(x, mods[0], row(norm1_g[0]), w0, False, tm_lat)
    yb_c, yb_l = _gdn(parts_c[1:], parts_l[1:], b_a_log[0], b_dt_bias[0], row(b_norm_g[0]))
    wm = {"lng": row(a_ln_g[0]), "lnb": row(a_ln_b[0]), "ws": a_ws[0].astype(bf16), "bst": a_bs[0].T,
          "wout": ab_w_out[0].astype(bf16)}
    cx = _mix0(ctx, parts_c[0], yb_c, mods[0], wm, True, tm_ctx)
    lat = _mix0(x, parts_l[0], yb_l, mods[0], wm, False, tm_lat)
    wf = _ffn_weights(ffn_w_up[0], ffn_conv_w[0], ffn_w_down[0])
    cx = _ffn(cx, mods[0], row(norm2_g[0]), wf, row(final_g), True, tm_ctx, False)
    lat = _ffn(lat, mods[0], row(norm2_g[0]), wf, row(final_g), False, tm_ffn, False)

    w_in = mla_w_in[0]
    kv0 = C_Q_LORA
    kr0 = C_Q_LORA + C_KV_LORA
    q_fold = (C_NOPE + C_ROPE) ** -0.5 * math.log2(math.e)
    w_uq = (mla_w_uq[0] * q_fold).reshape(C_Q_LORA, C_HEADS, C_NOPE + C_ROPE)
    w_ukv = mla_w_ukv[0].reshape(C_KV_LORA, C_HEADS, C_NOPE + C_VDIM)
    w_qr = w_uq[:, :, C_NOPE:]
    cos2, sin2 = _rope_tables(t_lat)
    w1 = {
        "win": jnp.concatenate([w_in, _swap_halves(w_in[:, kr0:])], axis=1).astype(bf16),
        "win_kv": jnp.pad(w_in[:, kv0:], ((0, 0), (0, C_ROPE))).astype(bf16),
        "qng": row(mla_q_norm_g[0]), "kvng": row(mla_kv_norm_g[0]),
        "wkn": w_ukv[:, :, :C_NOPE].reshape(C_KV_LORA, -1).astype(bf16),
        "wv": w_ukv[:, :, C_NOPE:].reshape(C_KV_LORA, -1).astype(bf16),
        "wqn": w_uq[:, :, :C_NOPE].reshape(C_Q_LORA, -1).astype(bf16),
        "wqr": w_qr.reshape(C_Q_LORA, -1).astype(bf16),
        "wqrs": _swap_halves(w_qr).reshape(C_Q_LORA, -1).astype(bf16),
        "cos2": cos2, "sin2": sin2,
    }
    kn_c, v_c, kr_c = _mla_proj(cx, mods[1], row(norm1_g[1]), w1, False, tm_ctx)
    kn_l, v_l, kr_l, qn, qr = _mla_proj(lat, mods[1], row(norm1_g[1]), w1, True, tm_lat)
    att = _attention(qn, qr, kn_c, kr_c, v_c, kn_l, kr_l, v_l, 256)
    lat = _proj_res(lat, att, mla_w_out[0].astype(bf16), mods[1], tm_lat)
    wf = _ffn_weights(ffn_w_up[1], ffn_conv_w[1], ffn_w_down[1])
    return _ffn(lat, mods[1], row(norm2_g[1]), wf, row(final_g), False, tm_ffn, True)
```

```python
import functools
import math

import jax
import jax.numpy as jnp
from jax import lax
from jax.experimental import pallas as pl
from jax.experimental.pallas import tpu as pltpu

f32 = jnp.float32
bf16 = jnp.bfloat16

D_MODEL = 1024
GRID_W = 64
A_GROUPS = 4
A_GROUP_DIM = 128
A_WIDTH = A_GROUPS * A_GROUP_DIM
A_CHUNK = 128
B_HEADS = 4
B_HEAD_DIM = 128
B_WIDTH = B_HEADS * B_HEAD_DIM
C_HEADS = 8
C_NOPE = 128
C_ROPE = 64
C_VDIM = 128
C_Q_LORA = 384
C_KV_LORA = 256
ROPE_THETA = 10000.0
D_FF = 2816
EPS = 1e-6

HALO = 8
GDN_CHUNK = 128
FFN_CW = 256
VMEM_LIMIT = 56 * 1024 * 1024

NT = (((1,), (1,)), ((), ()))


def _dot(a, b):
    return jnp.dot(a, b, preferred_element_type=f32)


def _silu(x):
    return x * jax.nn.sigmoid(x)


def _gelu(x):
    return 0.5 * x * (1.0 + lax.erf(x * (0.5 ** 0.5)))


def _softplus(x):
    return jnp.maximum(x, 0.0) + jnp.log1p(jnp.exp(-jnp.abs(x)))


def _rms(x, g):
    return x * lax.rsqrt(jnp.mean(x * x, axis=-1, keepdims=True) + EPS) * g


def _norm_mod(x, g, shift, scale):
    return _rms(x, g) * (1.0 + scale) + shift


def _split3(x):
    hi = x.astype(bf16)
    r = x - hi.astype(f32)
    mid = r.astype(bf16)
    lo = (r - mid.astype(f32)).astype(bf16)
    return hi, mid, lo


def _params(n_axes):
    return pltpu.CompilerParams(dimension_semantics=("arbitrary",) * n_axes,
                                vmem_limit_bytes=VMEM_LIMIT)


def _full(shape):
    nd = len(shape)
    return pl.BlockSpec(shape, lambda *_: (0,) * nd, pipeline_mode=pl.Buffered(1))


def _mod_spec(chunk, is_ctx, n_batch):
    if is_ctx:
        return pl.BlockSpec((None, None, 1, D_MODEL), lambda b, i: (n_batch, chunk, 0, 0))
    return pl.BlockSpec((None, None, 1, D_MODEL), lambda b, i: (b, chunk, 0, 0))


def _tok_spec(tm, width):
    return pl.BlockSpec((None, tm, width), lambda b, i: (b, i, 0))


def _halo_specs(tm, t_len):
    r = tm // HALO
    last = t_len // HALO - 1
    return [
        pl.BlockSpec((None, tm, D_MODEL), lambda b, i: (b, i, 0)),
        pl.BlockSpec((None, HALO, D_MODEL), lambda b, i: (b, jnp.maximum(i * r - 1, 0), 0)),
        pl.BlockSpec((None, HALO, D_MODEL), lambda b, i: (b, jnp.minimum((i + 1) * r, last), 0)),
    ]


def _fill_halo_h(x_ref, xp_ref, xn_ref, g_ref, sh_ref, sc_ref, hs_ref, tm):
    i = pl.program_id(1)
    nt = pl.num_programs(1)
    g, sh, sc = g_ref[...], sh_ref[...], sc_ref[...]
    hs_ref[pl.ds(HALO, tm), :] = _norm_mod(x_ref[...], g, sh, sc)
    hs_ref[pl.ds(0, HALO), :] = jnp.where(i > 0, _norm_mod(xp_ref[...], g, sh, sc), 0.0)
    hs_ref[pl.ds(HALO + tm, HALO), :] = jnp.where(i < nt - 1, _norm_mod(xn_ref[...], g, sh, sc), 0.0)


def _conv3(z_ref, cw, tm):
    return (cw[0:1] * z_ref[pl.ds(HALO - 1, tm), :] + cw[1:2] * z_ref[pl.ds(HALO, tm), :]
            + cw[2:3] * z_ref[pl.ds(HALO + 1, tm), :])


def _ada_kernel(c_ref, w_ref, b_ref, o_ref):
    s = _silu(c_ref[...]).astype(bf16)
    o_ref[...] = _dot(s, w_ref[...].astype(bf16)) + b_ref[...]


def _ada_mods(cc, ada_w, ada_b):
    depth = ada_w.shape[0]
    rows = cc.shape[0]
    return pl.pallas_call(
        _ada_kernel,
        out_shape=jax.ShapeDtypeStruct((depth, rows, 6 * D_MODEL), f32),
        grid=(depth, 6),
        in_specs=[pl.BlockSpec((rows, D_MODEL), lambda l, j: (0, 0)),
                  pl.BlockSpec((None, D_MODEL, D_MODEL), lambda l, j: (l, 0, j)),
                  pl.BlockSpec((None, 1, D_MODEL), lambda l, j: (l, 0, j))],
        out_specs=pl.BlockSpec((None, rows, D_MODEL), lambda l, j: (l, 0, j)),
        compiler_params=_params(2),
        name="ada_mods",
    )(cc, ada_w, ada_b.reshape(depth, 1, 6 * D_MODEL))


def _inproj0_kernel(x_ref, xp_ref, xn_ref, g_ref, sh_ref, sc_ref, wqkv_ref, wrest_ref, wab_ref,
                    wabt_ref, conv_ref, auav_ref, gate_ref, q_ref, k_ref, v_ref, ab_ref, abt_ref,
                    hs_ref, zs_ref, *, tm):
    _fill_halo_h(x_ref, xp_ref, xn_ref, g_ref, sh_ref, sc_ref, hs_ref, tm)
    hb_all = hs_ref[...].astype(bf16)
    hb = hs_ref[pl.ds(HALO, tm), :].astype(bf16)
    cols = lambda j: slice(j * B_WIDTH, (j + 1) * B_WIDTH)

    def proj(j):
        zs_ref[j % 2] = _dot(hb_all, wqkv_ref[:, cols(j)])

    def finish(j, o_ref):
        y = _silu(_conv3(zs_ref.at[j % 2], conv_ref[:, cols(j)], tm))
        if j == 2:
            o_ref[...] = y.astype(bf16)
            return
        post = B_HEAD_DIM ** -0.5 if j == 0 else 1.0
        for h in range(B_HEADS):
            yh = y[:, h * B_HEAD_DIM:(h + 1) * B_HEAD_DIM]
            inv = lax.rsqrt(jnp.sum(yh * yh, axis=-1, keepdims=True) + EPS) * post
            o_ref[:, h * B_HEAD_DIM:(h + 1) * B_HEAD_DIM] = (yh * inv).astype(bf16)

    proj(0)
    for j, o_ref in enumerate((q_ref, k_ref, v_ref)):
        if j < 2:
            proj(j + 1)
        finish(j, o_ref)
    auav_ref[:, 0:A_WIDTH] = _dot(hb, wrest_ref[:, 0:A_WIDTH]).astype(bf16)
    auav_ref[:, A_WIDTH:2 * A_WIDTH] = _dot(hb, wrest_ref[:, A_WIDTH:2 * A_WIDTH]).astype(bf16)
    gate_ref[...] = _dot(hb, wrest_ref[:, 2 * A_WIDTH:]).astype(bf16)
    ab_ref[...] = _dot(hb, wab_ref[...])[:, 0:4 * B_HEADS]
    abt_ref[...] = lax.dot_general(wabt_ref[...], hb, NT, preferred_element_type=f32)


def _inproj0(x, mods, norm_g, w, is_ctx, tm):
    n_batch, t_len, _ = x.shape
    nb = mods.shape[0] - 1
    tok = lambda width, dt: jax.ShapeDtypeStruct((n_batch, t_len, width), dt)
    return pl.pallas_call(
        functools.partial(_inproj0_kernel, tm=tm),
        out_shape=(tok(2 * A_WIDTH, bf16), tok(B_WIDTH, bf16), tok(B_WIDTH, bf16), tok(B_WIDTH, bf16),
                   tok(B_WIDTH, bf16), tok(4 * B_HEADS, f32),
                   jax.ShapeDtypeStruct((n_batch, 4 * B_HEADS, t_len), f32)),
        grid=(n_batch, t_len // tm),
        in_specs=_halo_specs(tm, t_len) + [
            _full((1, D_MODEL)), _mod_spec(0, is_ctx, nb), _mod_spec(1, is_ctx, nb),
            _full(w["wqkv"].shape), _full(w["wrest"].shape), _full(w["wab"].shape),
            _full(w["wabt"].shape), _full(w["conv"].shape)],
        out_specs=(_tok_spec(tm, 2 * A_WIDTH), _tok_spec(tm, B_WIDTH), _tok_spec(tm, B_WIDTH),
                   _tok_spec(tm, B_WIDTH), _tok_spec(tm, B_WIDTH), _tok_spec(tm, 4 * B_HEADS),
                   pl.BlockSpec((None, 4 * B_HEADS, tm), lambda b, i: (b, 0, i))),
        scratch_shapes=[pltpu.VMEM((tm + 2 * HALO, D_MODEL), f32),
                        pltpu.VMEM((2, tm + 2 * HALO, B_WIDTH), f32)],
        compiler_params=_params(2),
        name="inproj0_ctx" if is_ctx else "inproj0_lat",
    )(x, x, x, norm_g, mods, mods, w["wqkv"], w["wrest"], w["wab"], w["wabt"], w["conv"])


def _gdn_kernel(qc_ref, kc_ref, vc_ref, gc_ref, abc_ref, abtc_ref,
                ql_ref, kl_ref, vl_ref, gl_ref, abl_ref, abtl_ref,
                alc_ref, dtc_ref, alr_ref, dtr_ref, ng_ref,
                yc_ref, yl_ref,
                q_s, k_s, v_s, o_s, st_s, gcol_s, bcol_s, gtcol_s, grow_s, gtrow_s,
                *, t_ctx, t_lat):
    C = GDN_CHUNK
    n_ctx, n_lat = t_ctx // C, t_lat // C
    n_chunks = n_ctx + n_lat
    nh = B_HEADS

    q_s[pl.ds(0, t_ctx), :] = qc_ref[...]
    q_s[pl.ds(t_ctx, t_lat), :] = ql_ref[...]
    k_s[pl.ds(0, t_ctx), :] = kc_ref[...]
    k_s[pl.ds(t_ctx, t_lat), :] = kl_ref[...]
    v_s[pl.ds(0, t_ctx), :] = vc_ref[...]
    v_s[pl.ds(t_ctx, t_lat), :] = vl_ref[...]
    o_s[...] = jnp.zeros_like(o_s)
    st_s[...] = jnp.zeros_like(st_s)

    ri = lax.broadcasted_iota(jnp.int32, (C, C), 0)
    ci = lax.broadcasted_iota(jnp.int32, (C, C), 1)
    lower = (ri >= ci).astype(bf16)
    upper = (ri <= ci).astype(bf16)
    ones = jnp.ones((C, C), bf16)
    eye = (ri == ci).astype(f32)
    incl = (ri >= ci, ri <= ci)
    strict = (ri > ci, ri < ci)
    pair = ([], [])
    blk = 1
    while blk < C:
        same = (ri // (2 * blk)) == (ci // (2 * blk))
        hi_r, hi_c = (ri % (2 * blk)) >= blk, (ci % (2 * blk)) >= blk
        pair[0].append(same & hi_r & jnp.logical_not(hi_c))
        pair[1].append(same & hi_c & jnp.logical_not(hi_r))
        blk *= 2

    col_is_fwd = lax.broadcasted_iota(jnp.int32, (C, 4 * nh), 1) < nh
    row_is_fwd = lax.broadcasted_iota(jnp.int32, (2 * nh, C), 0) < nh
    for t in range(n_chunks):
        if t < n_ctx:
            ab = abc_ref[pl.ds(t * C, C), :]
            abt = abtc_ref[:, t * C:(t + 1) * C]
        else:
            ab = abl_ref[pl.ds((t - n_ctx) * C, C), :]
            abt = abtl_ref[:, (t - n_ctx) * C:(t - n_ctx + 1) * C]
        la = -jnp.exp(alc_ref[...]) * _softplus(ab + dtc_ref[...])
        p3 = _split3(la)
        pre = sum(_dot(lower, p) for p in p3)
        suf = sum(_dot(upper, p) for p in p3)
        gcol_s[t] = jnp.where(col_is_fwd, pre, suf)
        gtcol_s[t] = sum(_dot(ones, p) for p in p3)
        bcol_s[t] = jax.nn.sigmoid(ab)
        lar = -jnp.exp(alr_ref[...]) * _softplus(abt[0:2 * nh, :] + dtr_ref[...])
        r3 = _split3(lar)
        pre_r = sum(_dot(p, upper) for p in r3)
        suf_r = sum(_dot(p, lower) for p in r3)
        grow_s[t] = jnp.where(row_is_fwd, pre_r, suf_r)
        gtrow_s[t] = sum(_dot(p, ones) for p in r3)

    def chunk_load(t, d, h):
        col = d * nh + h
        r0 = pl.multiple_of(t * C, C)
        hs = slice(h * B_HEAD_DIM, (h + 1) * B_HEAD_DIM)
        return dict(
            q=q_s[pl.ds(r0, C), hs], k=k_s[pl.ds(r0, C), hs], v=v_s[pl.ds(r0, C), hs],
            gc=gcol_s[t][:, col:col + 1], gtc=gtcol_s[t][:, col:col + 1],
            bc=bcol_s[t][:, 2 * nh + col:2 * nh + col + 1],
            gr=grow_s[t][col:col + 1, :], gtr=gtrow_s[t][col:col + 1, :])

    def block_rows(s, d):
        return [(2 * m + 1 - d) * s for m in range(C // (2 * s))]

    def chunk_intra(x, d):
        q, k, v, gc, gtc, bc, gr, gtr = (x[n] for n in ("q", "k", "v", "gc", "gtc", "bc", "gr", "gtr"))
        qk_kk = lax.dot_general(jnp.concatenate([q, k], axis=0), k, NT, preferred_element_type=f32)
        yield
        qk, kk = qk_kk[0:C], qk_kk[C:2 * C]
        decay = jnp.where(incl[d], jnp.exp(jnp.where(incl[d], gc - gr, 0.0)), 0.0)
        a = jnp.where(strict[d], bc * kk * decay, 0.0)
        tinv = eye - jnp.where(pair[d][0], a, 0.0)
        for lvl in range(1, len(pair[d])):
            s = 2 ** lvl
            a_off = jnp.where(pair[d][lvl], a, 0.0).astype(bf16)
            tb16 = tinv.astype(bf16)
            if s < 8:
                ta = _dot(tb16, a_off).astype(bf16)
                yield
                tinv = tinv - _dot(ta, tb16)
                yield
                continue
            offs = block_rows(s, d)
            t_sel = jnp.concatenate([tinv[o:o + s] for o in offs], axis=0)
            ta = _dot(t_sel.astype(bf16), a_off).astype(bf16)
            yield
            t_sel = t_sel - _dot(ta, tb16)
            yield
            pieces, at = [], 0
            for m, o in enumerate(offs):
                pieces += [tinv[at:o], t_sel[m * s:(m + 1) * s]]
                at = o + s
            tinv = jnp.concatenate([p for p in pieces + [tinv[at:C]] if p.shape[0]], axis=0)
        e_gc = jnp.exp(gc)
        qf, kf, vf = q.astype(f32), k.astype(f32), v.astype(f32)
        vb_kbg = jnp.concatenate([vf * bc, kf * (bc * e_gc)], axis=1).astype(bf16)
        uw = _dot(tinv.astype(bf16), vb_kbg)
        yield
        yield dict(u=uw[:, 0:B_HEAD_DIM],
                   wq=jnp.concatenate([uw[:, B_HEAD_DIM:], qf * e_gc], axis=0).astype(bf16),
                   attn=(qk * decay).astype(bf16),
                   kdec_t=(kf.T * jnp.exp(gtr - gr)).astype(bf16),
                   gend=jnp.exp(gtc[0:1, :]))

    def chunk_state(p, s):
        wq_s = _dot(p["wq"], s.astype(bf16))
        yield
        v_new = (p["u"] - wq_s[0:C]).astype(bf16)
        yield s * p["gend"] + _dot(p["kdec_t"], v_new), wq_s[C:2 * C] + _dot(p["attn"], v_new)

    def round_robin(gens):
        while True:
            stage = [next(g) for g in gens]
            if stage[0] is not None:
                return stage

    steps = 2

    def body(j, carry):
        systems = []
        for st in range(steps):
            i = j * steps + st
            tb = jnp.where(i < n_ctx, n_ctx - 1 - i, n_chunks + n_ctx - 1 - i)
            systems += [(t, d, h) for h in range(nh) for d, t in ((0, i), (1, tb))]
        prods = round_robin([chunk_intra(chunk_load(t, d, h), d) for t, d, h in systems])
        state = [st_s[c] for c in range(2 * nh)]
        for st in range(steps):
            group = list(zip(systems, prods))[st * 2 * nh:(st + 1) * 2 * nh]
            outs = round_robin([chunk_state(p, state[d * nh + h]) for (_, d, h), p in group])
            for ((t, d, h), _), (s_new, o_add) in zip(group, outs):
                state[d * nh + h] = s_new
                o_s[pl.ds(pl.multiple_of(t * C, C), C), h * B_HEAD_DIM:(h + 1) * B_HEAD_DIM] += o_add
        for c in range(2 * nh):
            st_s[c] = state[c]
        return carry

    assert n_chunks % steps == 0
    lax.fori_loop(0, n_chunks // steps, body, 0)

    ng = ng_ref[...]
    for t in range(n_chunks):
        if t < n_ctx:
            gate, y_ref, rows = gc_ref[pl.ds(t * C, C), :], yc_ref, pl.ds(t * C, C)
        else:
            gate, y_ref, rows = gl_ref[pl.ds((t - n_ctx) * C, C), :], yl_ref, pl.ds((t - n_ctx) * C, C)
        o = o_s[pl.ds(t * C, C), :]
        gf = gate.astype(f32)
        for h in range(nh):
            hs = slice(h * B_HEAD_DIM, (h + 1) * B_HEAD_DIM)
            y_ref[rows, hs] = (_rms(o[:, hs], ng) * _silu(gf[:, hs])).astype(bf16)


def _gdn(ctx_parts, lat_parts, a_log, dt_bias, norm_g):
    gc, qc, kc, vc, abc, abtc = ctx_parts
    gl, ql, kl, vl, abl, abtl = lat_parts
    n_batch, t_ctx, _ = qc.shape
    t_lat = ql.shape[1]
    t_all = t_ctx + t_lat
    n_chunks = t_all // GDN_CHUNK
    nh = B_HEADS
    pad = jnp.zeros((2 * nh,), f32)
    alc = jnp.concatenate([a_log.reshape(-1), pad]).reshape(1, 4 * nh)
    dtc = jnp.concatenate([dt_bias.reshape(-1), pad]).reshape(1, 4 * nh)
    alr = jnp.broadcast_to(a_log.reshape(2 * nh, 1), (2 * nh, GDN_CHUNK))
    dtr = jnp.broadcast_to(dt_bias.reshape(2 * nh, 1), (2 * nh, GDN_CHUNK))
    seg = lambda t, wd: pl.BlockSpec((None, t, wd), lambda b: (b, 0, 0))
    segt = lambda t: pl.BlockSpec((None, 4 * nh, t), lambda b: (b, 0, 0))
    one = lambda shape: pl.BlockSpec(shape, lambda b: (0,) * len(shape))
    in_specs = ([seg(t_ctx, B_WIDTH)] * 4 + [seg(t_ctx, 4 * nh), segt(t_ctx)]
                + [seg(t_lat, B_WIDTH)] * 4 + [seg(t_lat, 4 * nh), segt(t_lat)]
                + [one((1, 4 * nh)), one((1, 4 * nh)), one((2 * nh, GDN_CHUNK)),
                   one((2 * nh, GDN_CHUNK)), one((1, B_HEAD_DIM))])
    return pl.pallas_call(
        functools.partial(_gdn_kernel, t_ctx=t_ctx, t_lat=t_lat),
        out_shape=(jax.ShapeDtypeStruct((n_batch, t_ctx, B_WIDTH), bf16),
                   jax.ShapeDtypeStruct((n_batch, t_lat, B_WIDTH), bf16)),
        grid=(n_batch,),
        in_specs=in_specs,
        out_specs=(seg(t_ctx, B_WIDTH), seg(t_lat, B_WIDTH)),
        scratch_shapes=[
            pltpu.VMEM((t_all, B_WIDTH), bf16), pltpu.VMEM((t_all, B_WIDTH), bf16),
            pltpu.VMEM((t_all, B_WIDTH), bf16), pltpu.VMEM((t_all, B_WIDTH), f32),
            pltpu.VMEM((2 * nh, B_HEAD_DIM, B_HEAD_DIM), f32),
            pltpu.VMEM((n_chunks, GDN_CHUNK, 4 * nh), f32), pltpu.VMEM((n_chunks, GDN_CHUNK, 4 * nh), f32),
            pltpu.VMEM((n_chunks, GDN_CHUNK, 4 * nh), f32),
            pltpu.VMEM((n_chunks, 2 * nh, GDN_CHUNK), f32), pltpu.VMEM((n_chunks, 2 * nh, GDN_CHUNK), f32)],
        compiler_params=_params(1),
        name="gdn",
    )(qc, kc, vc, gc, abc, abtc, ql, kl, vl, gl, abl, abtl, alc, dtc, alr, dtr, norm_g)


def _mix0_kernel(x_ref, auav_ref, yb_ref, lng_ref, lnb_ref, ws_ref, bst_ref, wout_ref, g1_ref,
                 o_ref, mix_ref, *, tm):
    au = auav_ref[:, 0:A_WIDTH].astype(f32)
    av = auav_ref[:, A_WIDTH:2 * A_WIDTH].astype(f32)
    u = _gelu(au)
    gv = _gelu(av)
    mu = jnp.mean(gv, axis=-1, keepdims=True)
    dv = gv - mu
    var = jnp.mean(dv * dv, axis=-1, keepdims=True)
    vn = (dv * lax.rsqrt(var + EPS) * lng_ref[...] + lnb_ref[...]).astype(bf16)
    for c in range(tm // A_CHUNK):
        rows = slice(c * A_CHUNK, (c + 1) * A_CHUNK)
        for g in range(A_GROUPS):
            cols = slice(g * A_GROUP_DIM, (g + 1) * A_GROUP_DIM)
            s = _dot(ws_ref[g], vn[rows, cols]) + bst_ref[:, g:g + 1]
            mix_ref[rows, cols] = (u[rows, cols] * s).astype(bf16)
    mix_ref[:, A_WIDTH:] = yb_ref[...]
    o_ref[...] = x_ref[...] + g1_ref[...] * _dot(mix_ref[...], wout_ref[...])


def _mix0(x, auav, yb, mods, w, is_ctx, tm):
    n_batch, t_len, _ = x.shape
    nb = mods.shape[0] - 1
    return pl.pallas_call(
        functools.partial(_mix0_kernel, tm=tm),
        out_shape=jax.ShapeDtypeStruct(x.shape, f32),
        grid=(n_batch, t_len // tm),
        in_specs=[_tok_spec(tm, D_MODEL), _tok_spec(tm, 2 * A_WIDTH), _tok_spec(tm, B_WIDTH),
                  _full((1, A_WIDTH)), _full((1, A_WIDTH)), _full(w["ws"].shape), _full(w["bst"].shape),
                  _full(w["wout"].shape), _mod_spec(2, is_ctx, nb)],
        out_specs=_tok_spec(tm, D_MODEL),
        scratch_shapes=[pltpu.VMEM((tm, A_WIDTH + B_WIDTH), bf16)],
        compiler_params=_params(2),
        name="mix0_ctx" if is_ctx else "mix0_lat",
    )(x, auav, yb, w["lng"], w["lnb"], w["ws"], w["bst"], w["wout"], mods)


def _ffn_kernel(x_ref, xp_ref, xn_ref, g_ref, sh_ref, sc_ref, gt_ref, wup_ref, conv_ref, wdn_ref, fg_ref,
                o_ref, hs_ref, zs_ref, act_ref, *, tm, final_norm):
    _fill_halo_h(x_ref, xp_ref, xn_ref, g_ref, sh_ref, sc_ref, hs_ref, tm)
    hb_all = hs_ref[...].astype(bf16)
    n = D_FF // FFN_CW
    gcols = lambda j: slice(j * FFN_CW, (j + 1) * FFN_CW)
    ucols = lambda j: slice(D_FF + j * FFN_CW, D_FF + (j + 1) * FFN_CW)

    def up(j):
        zs_ref[j % 2, 0] = _dot(hb_all, wup_ref[:, gcols(j)])
        zs_ref[j % 2, 1] = _dot(hb_all, wup_ref[:, ucols(j)])

    def act(j):
        g = _conv3(zs_ref.at[j % 2, 0], conv_ref[:, gcols(j)], tm)
        u = _conv3(zs_ref.at[j % 2, 1], conv_ref[:, ucols(j)], tm)
        act_ref[:, gcols(j)] = (_silu(g) * u).astype(bf16)

    up(0)
    for j in range(n):
        if j + 1 < n:
            up(j + 1)
        act(j)
    out = x_ref[...] + gt_ref[...] * _dot(act_ref[...], wdn_ref[...])
    if final_norm:
        out = _rms(out, fg_ref[...])
    o_ref[...] = out


def _ffn(x, mods, norm_g, w, final_g, is_ctx, tm, final_norm):
    n_batch, t_len, _ = x.shape
    nb = mods.shape[0] - 1
    return pl.pallas_call(
        functools.partial(_ffn_kernel, tm=tm, final_norm=final_norm),
        out_shape=jax.ShapeDtypeStruct(x.shape, f32),
        grid=(n_batch, t_len // tm),
        in_specs=_halo_specs(tm, t_len) + [
            _full((1, D_MODEL)), _mod_spec(3, is_ctx, nb), _mod_spec(4, is_ctx, nb), _mod_spec(5, is_ctx, nb),
            _full(w["wup"].shape), _full(w["conv"].shape), _full(w["wdn"].shape), _full((1, D_MODEL))],
        out_specs=_tok_spec(tm, D_MODEL),
        scratch_shapes=[pltpu.VMEM((tm + 2 * HALO, D_MODEL), f32),
                        pltpu.VMEM((2, 2, tm + 2 * HALO, FFN_CW), f32),
                        pltpu.VMEM((tm, D_FF), bf16)],
        compiler_params=_params(2),
        name="ffn_ctx" if is_ctx else "ffn_lat",
    )(x, x, x, norm_g, mods, mods, mods, w["wup"], w["conv"], w["wdn"], final_g)


def _mla_proj_kernel(*refs, with_q):
    if with_q:
        (x_ref, g_ref, sh_ref, sc_ref, win_ref, qng_ref, kvng_ref, wkn_ref, wv_ref,
         wqn_ref, wqr_ref, wqrs_ref, cos_ref, sin_ref,
         kn_ref, v_ref, kr_ref, qn_ref, qr_ref) = refs
    else:
        (x_ref, g_ref, sh_ref, sc_ref, win_ref, kvng_ref, wkn_ref, wv_ref,
         kn_ref, v_ref, kr_ref) = refs
    hb = _norm_mod(x_ref[...], g_ref[...], sh_ref[...], sc_ref[...]).astype(bf16)
    z = _dot(hb, win_ref[...])
    off = C_Q_LORA if with_q else 0
    ckvn = _rms(z[:, off:off + C_KV_LORA], kvng_ref[...]).astype(bf16)
    kn_ref[...] = _dot(ckvn, wkn_ref[...]).astype(bf16)
    v_ref[...] = _dot(ckvn, wv_ref[...]).astype(bf16)
    kr = z[:, off + C_KV_LORA:off + C_KV_LORA + C_ROPE]
    if not with_q:
        kr_ref[...] = kr.astype(bf16)
        return
    krs = z[:, off + C_KV_LORA + C_ROPE:off + C_KV_LORA + 2 * C_ROPE]
    cos2, sin2 = cos_ref[...], sin_ref[...]
    kr_ref[...] = (kr * cos2[:, 0:C_ROPE] + krs * sin2[:, 0:C_ROPE]).astype(bf16)
    cqn = _rms(z[:, 0:C_Q_LORA], qng_ref[...]).astype(bf16)
    qn_ref[...] = _dot(cqn, wqn_ref[...]).astype(bf16)
    qr = _dot(cqn, wqr_ref[...])
    qrs = _dot(cqn, wqrs_ref[...])
    for hp in range(C_HEADS // 2):
        cols = slice(hp * 2 * C_ROPE, (hp + 1) * 2 * C_ROPE)
        rot = (qr[:, cols] * cos2 + qrs[:, cols] * sin2).astype(bf16)
        qr_ref[2 * hp] = rot[:, 0:C_ROPE]
        qr_ref[2 * hp + 1] = rot[:, C_ROPE:]


def _mla_proj(x, mods, norm_g, w, with_q, tm):
    n_batch, t_len, _ = x.shape
    nb = mods.shape[0] - 1
    is_ctx = not with_q
    hw = C_HEADS * C_NOPE
    tok = lambda width: jax.ShapeDtypeStruct((n_batch, t_len, width), bf16)
    out_shape = [tok(hw), tok(C_HEADS * C_VDIM), tok(C_ROPE)]
    out_specs = [_tok_spec(tm, hw), _tok_spec(tm, C_HEADS * C_VDIM), _tok_spec(tm, C_ROPE)]
    in_specs = [_tok_spec(tm, D_MODEL), _full((1, D_MODEL)), _mod_spec(0, is_ctx, nb), _mod_spec(1, is_ctx, nb)]
    if with_q:
        args = [w["win"], w["qng"], w["kvng"], w["wkn"], w["wv"], w["wqn"], w["wqr"], w["wqrs"]]
        in_specs += [_full(a.shape) for a in args]
        in_specs += [pl.BlockSpec((tm, 2 * C_ROPE), lambda b, i: (i, 0))] * 2
        args += [w["cos2"], w["sin2"]]
        out_shape += [tok(hw), jax.ShapeDtypeStruct((n_batch, C_HEADS, t_len, C_ROPE), bf16)]
        out_specs += [_tok_spec(tm, hw), pl.BlockSpec((None, C_HEADS, tm, C_ROPE), lambda b, i: (b, 0, i, 0))]
    else:
        args = [w["win_kv"], w["kvng"], w["wkn"], w["wv"]]
        in_specs += [_full(a.shape) for a in args]
    return pl.pallas_call(
        functools.partial(_mla_proj_kernel, with_q=with_q),
        out_shape=tuple(out_shape),
        grid=(n_batch, t_len // tm),
        in_specs=in_specs,
        out_specs=tuple(out_specs),
        compiler_params=_params(2),
        name="mla_proj_lat" if with_q else "mla_proj_ctx",
    )(x, norm_g, mods, mods, *args)


def _attn_kernel(qn_ref, qr_ref, knc_ref, krc_ref, vc_ref, knl_ref, krl_ref, vl_ref, o_ref,
                 kc_s, kl_s, q_s, *, tq):
    pad = C_NOPE - C_ROPE
    for dst, n_ref, r_ref in ((kc_s, knc_ref, krc_ref), (kl_s, knl_ref, krl_ref), (q_s, qn_ref, qr_ref)):
        dst[:, 0:C_NOPE] = n_ref[...]
        dst[:, C_NOPE:C_NOPE + C_ROPE] = r_ref[...]
        dst[:, C_NOPE + C_ROPE:] = jnp.zeros((dst.shape[0], pad), bf16)
    n_tiles = q_s.shape[0] // tq

    def scores(i):
        q = q_s[pl.ds(i * tq, tq), :]
        return (lax.dot_general(q, kc_s[...], NT, preferred_element_type=f32),
                lax.dot_general(q, kl_s[...], NT, preferred_element_type=f32))

    def finish(i, sc, sl):
        m = jnp.maximum(jnp.max(sc, axis=-1, keepdims=True), jnp.max(sl, axis=-1, keepdims=True))
        pc = jnp.exp2(sc - m)
        pl_ = jnp.exp2(sl - m)
        denom = jnp.sum(pc, axis=-1, keepdims=True) + jnp.sum(pl_, axis=-1, keepdims=True)
        o = _dot(pc.astype(bf16), vc_ref[...]) + _dot(pl_.astype(bf16), vl_ref[...])
        o_ref[pl.ds(i * tq, tq), :] = (o / denom).astype(bf16)

    nxt = scores(0)
    for i in range(n_tiles):
        cur = nxt
        if i + 1 < n_tiles:
            nxt = scores(i + 1)
        finish(i, *cur)


def _attention(qn, qr, kn_c, kr_c, v_c, kn_l, kr_l, v_l, tq):
    n_batch, t_lat, _ = qn.shape
    t_ctx = kn_c.shape[1]
    head = lambda t: pl.BlockSpec((None, t, C_NOPE), lambda b, h: (b, 0, h))
    rope = lambda t: pl.BlockSpec((None, t, C_ROPE), lambda b, h: (b, 0, 0))
    return pl.pallas_call(
        functools.partial(_attn_kernel, tq=tq),
        out_shape=jax.ShapeDtypeStruct((n_batch, t_lat, C_HEADS * C_VDIM), bf16),
        grid=(n_batch, C_HEADS),
        in_specs=[head(t_lat), pl.BlockSpec((None, None, t_lat, C_ROPE), lambda b, h: (b, h, 0, 0)),
                  head(t_ctx), rope(t_ctx), head(t_ctx), head(t_lat), rope(t_lat), head(t_lat)],
        out_specs=head(t_lat),
        scratch_shapes=[pltpu.VMEM((t_ctx, 2 * C_NOPE), bf16), pltpu.VMEM((t_lat, 2 * C_NOPE), bf16),
                        pltpu.VMEM((t_lat, 2 * C_NOPE), bf16)],
        compiler_params=_params(2),
        name="mla_attention",
    )(qn, qr, kn_c, kr_c, v_c, kn_l, kr_l, v_l)


def _proj_res_kernel(x_ref, y_ref, w_ref, g1_ref, o_ref):
    o_ref[...] = x_ref[...] + g1_ref[...] * _dot(y_ref[...], w_ref[...])


def _proj_res(x, y, w_out, mods, tm):
    n_batch, t_len, _ = x.shape
    nb = mods.shape[0] - 1
    return pl.pallas_call(
        _proj_res_kernel,
        out_shape=jax.ShapeDtypeStruct(x.shape, f32),
        grid=(n_batch, t_len // tm),
        in_specs=[_tok_spec(tm, D_MODEL), _tok_spec(tm, y.shape[-1]), _full(w_out.shape),
                  _mod_spec(2, False, nb)],
        out_specs=_tok_spec(tm, D_MODEL),
        compiler_params=_params(2),
        name="mla_out_proj",
    )(x, y, w_out, mods)


def _rope_tables(n):
    rows = n // GRID_W
    row = jnp.repeat(jnp.arange(rows, dtype=f32), GRID_W)
    col = jnp.tile(jnp.arange(GRID_W, dtype=f32), rows)
    n_freq = C_ROPE // 4
    inv = ROPE_THETA ** (-jnp.arange(n_freq, dtype=f32) / n_freq)
    ang = jnp.concatenate([row[:, None] * inv, col[:, None] * inv], axis=-1)
    cos, sin = jnp.cos(ang), jnp.sin(ang)
    cos64 = jnp.concatenate([cos, cos], axis=-1)
    sin64 = jnp.concatenate([-sin, sin], axis=-1)
    return jnp.tile(cos64, (1, 2)), jnp.tile(sin64, (1, 2))


def _swap_halves(w):
    h = C_ROPE // 2
    return jnp.concatenate([w[..., h:], w[..., :h]], axis=-1)


def _ffn_weights(w_up, conv_w, w_down):
    return {"wup": w_up.astype(bf16), "conv": conv_w, "wdn": w_down.astype(bf16)}


def kernel(x, c, ctx, c_ctx, ada_w, ada_b, norm1_g, norm2_g, ab_w_in, a_ln_g, a_ln_b, a_ws, a_bs, b_conv_w, b_a_log, b_dt_bias, b_norm_g, ab_w_out, mla_w_in, mla_q_norm_g, mla_kv_norm_g, mla_w_uq, mla_w_ukv, mla_w_out, ffn_w_up, ffn_conv_w, ffn_w_down, final_g):
    n_batch, t_lat, d = x.shape
    t_ctx = ctx.shape[1]
    tm_lat, tm_ctx = 512, t_ctx

    cc = jnp.concatenate([c, c_ctx[None, :], jnp.zeros((7, d), f32)], axis=0)
    mods_all = _ada_mods(cc, ada_w, ada_b)
    mods = [mods_all[i, :n_batch + 1].reshape(n_batch + 1, 6, 1, d) for i in range(2)]
    row = lambda v: v.reshape(1, -1)

    a2, b4 = 2 * A_WIDTH, 4 * B_WIDTH
    w_in = ab_w_in[0]
    w_ab = w_in[:, a2 + b4:]
    w0 = {
        "wqkv": w_in[:, a2:a2 + 3 * B_WIDTH].astype(bf16),
        "wrest": jnp.concatenate([w_in[:, :a2], w_in[:, a2 + 3 * B_WIDTH:a2 + b4]], axis=1).astype(bf16),
        "wab": jnp.pad(w_ab, ((0, 0), (0, 128 - 4 * B_HEADS))).astype(bf16),
        "wabt": w_ab.T.astype(bf16),
        "conv": b_conv_w[0],
    }
    parts_c = _inproj0(ctx, mods[0], row(norm1_g[0]), w0, True, tm_ctx)
    parts_l = _inproj0(x, mods[0], row(norm1_g[0]), w0, False, tm_lat)
    yb_c, yb_l = _gdn(parts_c[1:], parts_l[1:], b_a_log[0], b_dt_bias[0], row(b_norm_g[0]))
    wm = {"lng": row(a_ln_g[0]), "lnb": row(a_ln_b[0]), "ws": a_ws[0].astype(bf16), "bst": a_bs[0].T,
          "wout": ab_w_out[0].astype(bf16)}
    cx = _mix0(ctx, parts_c[0], yb_c, mods[0], wm, True, tm_ctx)
    lat = _mix0(x, parts_l[0], yb_l, mods[0], wm, False, tm_lat)
    wf = _ffn_weights(ffn_w_up[0], ffn_conv_w[0], ffn_w_down[0])
    cx = _ffn(cx, mods[0], row(norm2_g[0]), wf, row(final_g), True, tm_ctx, False)
    lat = _ffn(lat, mods[0], row(norm2_g[0]), wf, row(final_g), False, tm_lat, False)

    w_in = mla_w_in[0]
    kv0 = C_Q_LORA
    kr0 = C_Q_LORA + C_KV_LORA
    q_fold = (C_NOPE + C_ROPE) ** -0.5 * math.log2(math.e)
    w_uq = (mla_w_uq[0] * q_fold).reshape(C_Q_LORA, C_HEADS, C_NOPE + C_ROPE)
    w_ukv = mla_w_ukv[0].reshape(C_KV_LORA, C_HEADS, C_NOPE + C_VDIM)
    w_qr = w_uq[:, :, C_NOPE:]
    cos2, sin2 = _rope_tables(t_lat)
    w1 = {
        "win": jnp.concatenate([w_in, _swap_halves(w_in[:, kr0:])], axis=1).astype(bf16),
        "win_kv": jnp.pad(w_in[:, kv0:], ((0, 0), (0, C_ROPE))).astype(bf16),
        "qng": row(mla_q_norm_g[0]), "kvng": row(mla_kv_norm_g[0]),
        "wkn": w_ukv[:, :, :C_NOPE].reshape(C_KV_LORA, -1).astype(bf16),
        "wv": w_ukv[:, :, C_NOPE:].reshape(C_KV_LORA, -1).astype(bf16),
        "wqn": w_uq[:, :, :C_NOPE].reshape(C_Q_LORA, -1).astype(bf16),
        "wqr": w_qr.reshape(C_Q_LORA, -1).astype(bf16),
        "wqrs": _swap_halves(w_qr).reshape(C_Q_LORA, -1).astype(bf16),
        "cos2": cos2, "sin2": sin2,
    }
    kn_c, v_c, kr_c = _mla_proj(cx, mods[1], row(norm1_g[1]), w1, False, tm_ctx)
    kn_l, v_l, kr_l, qn, qr = _mla_proj(lat, mods[1], row(norm1_g[1]), w1, True, tm_lat)
    att = _attention(qn, qr, kn_c, kr_c, v_c, kn_l, kr_l, v_l, 256)
    lat = _proj_res(lat, att, mla_w_out[0].astype(bf16), mods[1], tm_lat)
    wf = _ffn_weights(ffn_w_up[1], ffn_conv_w[1], ffn_w_down[1])
    return _ffn(lat, mods[1], row(norm2_g[1]), wf, row(final_g), False, tm_lat, True)
```

```python
import functools
import math

import jax
import jax.numpy as jnp
from jax import lax
from jax.experimental import pallas as pl
from jax.experimental.pallas import tpu as pltpu

f32 = jnp.float32
bf16 = jnp.bfloat16

D_MODEL = 1024
GRID_W = 64
A_GROUPS = 4
A_GROUP_DIM = 128
A_WIDTH = A_GROUPS * A_GROUP_DIM
A_CHUNK = 128
B_HEADS = 4
B_HEAD_DIM = 128
B_WIDTH = B_HEADS * B_HEAD_DIM
C_HEADS = 8
C_NOPE = 128
C_ROPE = 64
C_VDIM = 128
C_Q_LORA = 384
C_KV_LORA = 256
ROPE_THETA = 10000.0
D_FF = 2816
EPS = 1e-6

HALO = 8
GDN_CHUNK = 128
FFN_CW = 256
VMEM_LIMIT = 56 * 1024 * 1024

NT = (((1,), (1,)), ((), ()))


def _dot(a, b):
    return jnp.dot(a, b, preferred_element_type=f32)


def _silu(x):
    return x * jax.nn.sigmoid(x)


def _gelu(x):
    return 0.5 * x * (1.0 + lax.erf(x * (0.5 ** 0.5)))


def _softplus(x):
    return jnp.maximum(x, 0.0) + jnp.log1p(jnp.exp(-jnp.abs(x)))


def _rms(x, g):
    return x * lax.rsqrt(jnp.mean(x * x, axis=-1, keepdims=True) + EPS) * g


def _norm_mod(x, g, shift, scale):
    return _rms(x, g) * (1.0 + scale) + shift


def _split3(x):
    hi = x.astype(bf16)
    r = x - hi.astype(f32)
    mid = r.astype(bf16)
    lo = (r - mid.astype(f32)).astype(bf16)
    return hi, mid, lo


def _params(n_axes):
    return pltpu.CompilerParams(dimension_semantics=("arbitrary",) * n_axes,
                                vmem_limit_bytes=VMEM_LIMIT)


def _full(shape):
    nd = len(shape)
    return pl.BlockSpec(shape, lambda *_: (0,) * nd, pipeline_mode=pl.Buffered(1))


def _mod_spec(chunk, is_ctx, n_batch):
    if is_ctx:
        return pl.BlockSpec((None, None, 1, D_MODEL), lambda b, i: (n_batch, chunk, 0, 0))
    return pl.BlockSpec((None, None, 1, D_MODEL), lambda b, i: (b, chunk, 0, 0))


def _tok_spec(tm, width):
    return pl.BlockSpec((None, tm, width), lambda b, i: (b, i, 0))


def _halo_specs(tm, t_len):
    r = tm // HALO
    last = t_len // HALO - 1
    return [
        pl.BlockSpec((None, tm, D_MODEL), lambda b, i: (b, i, 0)),
        pl.BlockSpec((None, HALO, D_MODEL), lambda b, i: (b, jnp.maximum(i * r - 1, 0), 0)),
        pl.BlockSpec((None, HALO, D_MODEL), lambda b, i: (b, jnp.minimum((i + 1) * r, last), 0)),
    ]


def _fill_halo_h(x_ref, xp_ref, xn_ref, g_ref, sh_ref, sc_ref, hs_ref, tm):
    i = pl.program_id(1)
    nt = pl.num_programs(1)
    g, sh, sc = g_ref[...], sh_ref[...], sc_ref[...]
    hs_ref[pl.ds(HALO, tm), :] = _norm_mod(x_ref[...], g, sh, sc)
    hs_ref[pl.ds(0, HALO), :] = jnp.where(i > 0, _norm_mod(xp_ref[...], g, sh, sc), 0.0)
    hs_ref[pl.ds(HALO + tm, HALO), :] = jnp.where(i < nt - 1, _norm_mod(xn_ref[...], g, sh, sc), 0.0)


def _conv3(z_ref, cw, tm):
    return (cw[0:1] * z_ref[pl.ds(HALO - 1, tm), :] + cw[1:2] * z_ref[pl.ds(HALO, tm), :]
            + cw[2:3] * z_ref[pl.ds(HALO + 1, tm), :])


def _ada_kernel(c_ref, w_ref, b_ref, o_ref):
    s = _silu(c_ref[...]).astype(bf16)
    o_ref[...] = _dot(s, w_ref[...].astype(bf16)) + b_ref[...]


def _ada_mods(cc, ada_w, ada_b):
    depth = ada_w.shape[0]
    rows = cc.shape[0]
    return pl.pallas_call(
        _ada_kernel,
        out_shape=jax.ShapeDtypeStruct((depth, rows, 6 * D_MODEL), f32),
        grid=(depth, 6),
        in_specs=[pl.BlockSpec((rows, D_MODEL), lambda l, j: (0, 0)),
                  pl.BlockSpec((None, D_MODEL, D_MODEL), lambda l, j: (l, 0, j)),
                  pl.BlockSpec((None, 1, D_MODEL), lambda l, j: (l, 0, j))],
        out_specs=pl.BlockSpec((None, rows, D_MODEL), lambda l, j: (l, 0, j)),
        compiler_params=_params(2),
        name="ada_mods",
    )(cc, ada_w, ada_b.reshape(depth, 1, 6 * D_MODEL))


def _inproj0_kernel(x_ref, xp_ref, xn_ref, g_ref, sh_ref, sc_ref, wqkv_ref, wrest_ref, wab_ref,
                    wabt_ref, conv_ref, auav_ref, gate_ref, q_ref, k_ref, v_ref, ab_ref, abt_ref,
                    hs_ref, zs_ref, *, tm):
    _fill_halo_h(x_ref, xp_ref, xn_ref, g_ref, sh_ref, sc_ref, hs_ref, tm)
    hb_all = hs_ref[...].astype(bf16)
    hb = hs_ref[pl.ds(HALO, tm), :].astype(bf16)
    cols = lambda j: slice(j * B_WIDTH, (j + 1) * B_WIDTH)

    def proj(j):
        zs_ref[j % 2] = _dot(hb_all, wqkv_ref[:, cols(j)])

    def finish(j, o_ref):
        y = _silu(_conv3(zs_ref.at[j % 2], conv_ref[:, cols(j)], tm))
        if j == 2:
            o_ref[...] = y.astype(bf16)
            return
        post = B_HEAD_DIM ** -0.5 if j == 0 else 1.0
        for h in range(B_HEADS):
            yh = y[:, h * B_HEAD_DIM:(h + 1) * B_HEAD_DIM]
            inv = lax.rsqrt(jnp.sum(yh * yh, axis=-1, keepdims=True) + EPS) * post
            o_ref[:, h * B_HEAD_DIM:(h + 1) * B_HEAD_DIM] = (yh * inv).astype(bf16)

    proj(0)
    for j, o_ref in enumerate((q_ref, k_ref, v_ref)):
        if j < 2:
            proj(j + 1)
        finish(j, o_ref)
    auav_ref[:, 0:A_WIDTH] = _dot(hb, wrest_ref[:, 0:A_WIDTH]).astype(bf16)
    auav_ref[:, A_WIDTH:2 * A_WIDTH] = _dot(hb, wrest_ref[:, A_WIDTH:2 * A_WIDTH]).astype(bf16)
    gate_ref[...] = _dot(hb, wrest_ref[:, 2 * A_WIDTH:]).astype(bf16)
    ab_ref[...] = _dot(hb, wab_ref[...])[:, 0:4 * B_HEADS]
    abt_ref[...] = lax.dot_general(wabt_ref[...], hb, NT, preferred_element_type=f32)


def _inproj0(x, mods, norm_g, w, is_ctx, tm):
    n_batch, t_len, _ = x.shape
    nb = mods.shape[0] - 1
    tok = lambda width, dt: jax.ShapeDtypeStruct((n_batch, t_len, width), dt)
    return pl.pallas_call(
        functools.partial(_inproj0_kernel, tm=tm),
        out_shape=(tok(2 * A_WIDTH, bf16), tok(B_WIDTH, bf16), tok(B_WIDTH, bf16), tok(B_WIDTH, bf16),
                   tok(B_WIDTH, bf16), tok(4 * B_HEADS, f32),
                   jax.ShapeDtypeStruct((n_batch, 4 * B_HEADS, t_len), f32)),
        grid=(n_batch, t_len // tm),
        in_specs=_halo_specs(tm, t_len) + [
            _full((1, D_MODEL)), _mod_spec(0, is_ctx, nb), _mod_spec(1, is_ctx, nb),
            _full(w["wqkv"].shape), _full(w["wrest"].shape), _full(w["wab"].shape),
            _full(w["wabt"].shape), _full(w["conv"].shape)],
        out_specs=(_tok_spec(tm, 2 * A_WIDTH), _tok_spec(tm, B_WIDTH), _tok_spec(tm, B_WIDTH),
                   _tok_spec(tm, B_WIDTH), _tok_spec(tm, B_WIDTH), _tok_spec(tm, 4 * B_HEADS),
                   pl.BlockSpec((None, 4 * B_HEADS, tm), lambda b, i: (b, 0, i))),
        scratch_shapes=[pltpu.VMEM((tm + 2 * HALO, D_MODEL), f32),
                        pltpu.VMEM((2, tm + 2 * HALO, B_WIDTH), f32)],
        compiler_params=_params(2),
        name="inproj0_ctx" if is_ctx else "inproj0_lat",
    )(x, x, x, norm_g, mods, mods, w["wqkv"], w["wrest"], w["wab"], w["wabt"], w["conv"])


def _gdn_kernel(qc_ref, kc_ref, vc_ref, gc_ref, abc_ref, abtc_ref,
                ql_ref, kl_ref, vl_ref, gl_ref, abl_ref, abtl_ref,
                alc_ref, dtc_ref, alr_ref, dtr_ref, ng_ref,
                yc_ref, yl_ref,
                q_s, k_s, v_s, o_s, st_s, gcol_s, bcol_s, gtcol_s, grow_s, gtrow_s,
                *, t_ctx, t_lat):
    C = GDN_CHUNK
    n_ctx, n_lat = t_ctx // C, t_lat // C
    n_chunks = n_ctx + n_lat
    nh = B_HEADS

    q_s[pl.ds(0, t_ctx), :] = qc_ref[...]
    q_s[pl.ds(t_ctx, t_lat), :] = ql_ref[...]
    k_s[pl.ds(0, t_ctx), :] = kc_ref[...]
    k_s[pl.ds(t_ctx, t_lat), :] = kl_ref[...]
    v_s[pl.ds(0, t_ctx), :] = vc_ref[...]
    v_s[pl.ds(t_ctx, t_lat), :] = vl_ref[...]
    o_s[...] = jnp.zeros_like(o_s)
    st_s[...] = jnp.zeros_like(st_s)

    ri = lax.broadcasted_iota(jnp.int32, (C, C), 0)
    ci = lax.broadcasted_iota(jnp.int32, (C, C), 1)
    lower = (ri >= ci).astype(bf16)
    upper = (ri <= ci).astype(bf16)
    ones = jnp.ones((C, C), bf16)
    eye = (ri == ci).astype(f32)
    incl = (ri >= ci, ri <= ci)
    strict = (ri > ci, ri < ci)
    pair = ([], [])
    blk = 1
    while blk < C:
        same = (ri // (2 * blk)) == (ci // (2 * blk))
        hi_r, hi_c = (ri % (2 * blk)) >= blk, (ci % (2 * blk)) >= blk
        pair[0].append(same & hi_r & jnp.logical_not(hi_c))
        pair[1].append(same & hi_c & jnp.logical_not(hi_r))
        blk *= 2

    col_is_fwd = lax.broadcasted_iota(jnp.int32, (C, 4 * nh), 1) < nh
    row_is_fwd = lax.broadcasted_iota(jnp.int32, (2 * nh, C), 0) < nh
    for t in range(n_chunks):
        if t < n_ctx:
            ab = abc_ref[pl.ds(t * C, C), :]
            abt = abtc_ref[:, t * C:(t + 1) * C]
        else:
            ab = abl_ref[pl.ds((t - n_ctx) * C, C), :]
            abt = abtl_ref[:, (t - n_ctx) * C:(t - n_ctx + 1) * C]
        la = -jnp.exp(alc_ref[...]) * _softplus(ab + dtc_ref[...])
        p3 = _split3(la)
        pre = sum(_dot(lower, p) for p in p3)
        suf = sum(_dot(upper, p) for p in p3)
        gcol_s[t] = jnp.where(col_is_fwd, pre, suf)
        gtcol_s[t] = sum(_dot(ones, p) for p in p3)
        bcol_s[t] = jax.nn.sigmoid(ab)
        lar = -jnp.exp(alr_ref[...]) * _softplus(abt[0:2 * nh, :] + dtr_ref[...])
        r3 = _split3(lar)
        pre_r = sum(_dot(p, upper) for p in r3)
        suf_r = sum(_dot(p, lower) for p in r3)
        grow_s[t] = jnp.where(row_is_fwd, pre_r, suf_r)
        gtrow_s[t] = sum(_dot(p, ones) for p in r3)

    def chunk_load(t, d, h):
        col = d * nh + h
        r0 = pl.multiple_of(t * C, C)
        hs = slice(h * B_HEAD_DIM, (h + 1) * B_HEAD_DIM)
        return dict(
            q=q_s[pl.ds(r0, C), hs], k=k_s[pl.ds(r0, C), hs], v=v_s[pl.ds(r0, C), hs],
            gc=gcol_s[t][:, col:col + 1], gtc=gtcol_s[t][:, col:col + 1],
            bc=bcol_s[t][:, 2 * nh + col:2 * nh + col + 1],
            gr=grow_s[t][col:col + 1, :], gtr=gtrow_s[t][col:col + 1, :])

    def block_rows(s, d):
        return [(2 * m + 1 - d) * s for m in range(C // (2 * s))]

    def chunk_intra(x, d):
        q, k, v, gc, gtc, bc, gr, gtr = (x[n] for n in ("q", "k", "v", "gc", "gtc", "bc", "gr", "gtr"))
        qk_kk = lax.dot_general(jnp.concatenate([q, k], axis=0), k, NT, preferred_element_type=f32)
        yield
        qk, kk = qk_kk[0:C], qk_kk[C:2 * C]
        decay = jnp.where(incl[d], jnp.exp(jnp.where(incl[d], gc - gr, 0.0)), 0.0)
        a = jnp.where(strict[d], bc * kk * decay, 0.0)
        tinv = eye - jnp.where(pair[d][0], a, 0.0)
        for lvl in range(1, len(pair[d])):
            s = 2 ** lvl
            a_off = jnp.where(pair[d][lvl], a, 0.0).astype(bf16)
            tb16 = tinv.astype(bf16)
            if s < 8:
                ta = _dot(tb16, a_off).astype(bf16)
                yield
                tinv = tinv - _dot(ta, tb16)
                yield
                continue
            offs = block_rows(s, d)
            t_sel = jnp.concatenate([tinv[o:o + s] for o in offs], axis=0)
            ta = _dot(t_sel.astype(bf16), a_off).astype(bf16)
            yield
            t_sel = t_sel - _dot(ta, tb16)
            yield
            pieces, at = [], 0
            for m, o in enumerate(offs):
                pieces += [tinv[at:o], t_sel[m * s:(m + 1) * s]]
                at = o + s
            tinv = jnp.concatenate([p for p in pieces + [tinv[at:C]] if p.shape[0]], axis=0)
        e_gc = jnp.exp(gc)
        qf, kf, vf = q.astype(f32), k.astype(f32), v.astype(f32)
        vb_kbg = jnp.concatenate([vf * bc, kf * (bc * e_gc)], axis=1).astype(bf16)
        uw = _dot(tinv.astype(bf16), vb_kbg)
        yield
        yield dict(u=uw[:, 0:B_HEAD_DIM],
                   wq=jnp.concatenate([uw[:, B_HEAD_DIM:], qf * e_gc], axis=0).astype(bf16),
                   attn=(qk * decay).astype(bf16),
                   kdec_t=(kf.T * jnp.exp(gtr - gr)).astype(bf16),
                   gend=jnp.exp(gtc[0:1, :]))

    def chunk_state(p, s):
        wq_s = _dot(p["wq"], s.astype(bf16))
        yield
        v_new = (p["u"] - wq_s[0:C]).astype(bf16)
        yield s * p["gend"] + _dot(p["kdec_t"], v_new), wq_s[C:2 * C] + _dot(p["attn"], v_new)

    def round_robin(gens):
        while True:
            stage = [next(g) for g in gens]
            if stage[0] is not None:
                return stage

    steps = 2

    def body(j, carry):
        systems = []
        for st in range(steps):
            i = j * steps + st
            tb = jnp.where(i < n_ctx, n_ctx - 1 - i, n_chunks + n_ctx - 1 - i)
            systems += [(t, d, h) for h in range(nh) for d, t in ((0, i), (1, tb))]
        prods = round_robin([chunk_intra(chunk_load(t, d, h), d) for t, d, h in systems])
        state = [st_s[c] for c in range(2 * nh)]
        for st in range(steps):
            group = list(zip(systems, prods))[st * 2 * nh:(st + 1) * 2 * nh]
            outs = round_robin([chunk_state(p, state[d * nh + h]) for (_, d, h), p in group])
            for ((t, d, h), _), (s_new, o_add) in zip(group, outs):
                state[d * nh + h] = s_new
                o_s[pl.ds(pl.multiple_of(t * C, C), C), h * B_HEAD_DIM:(h + 1) * B_HEAD_DIM] += o_add
        for c in range(2 * nh):
            st_s[c] = state[c]
        return carry

    assert n_chunks % steps == 0
    lax.fori_loop(0, n_chunks // steps, body, 0)

    ng = ng_ref[...]
    for t in range(n_chunks):
        if t < n_ctx:
            gate, y_ref, rows = gc_ref[pl.ds(t * C, C), :], yc_ref, pl.ds(t * C, C)
        else:
            gate, y_ref, rows = gl_ref[pl.ds((t - n_ctx) * C, C), :], yl_ref, pl.ds((t - n_ctx) * C, C)
        o = o_s[pl.ds(t * C, C), :]
        gf = gate.astype(f32)
        for h in range(nh):
            hs = slice(h * B_HEAD_DIM, (h + 1) * B_HEAD_DIM)
            y_ref[rows, hs] = (_rms(o[:, hs], ng) * _silu(gf[:, hs])).astype(bf16)


def _gdn(ctx_parts, lat_parts, a_log, dt_bias, norm_g):
    gc, qc, kc, vc, abc, abtc = ctx_parts
    gl, ql, kl, vl, abl, abtl = lat_parts
    n_batch, t_ctx, _ = qc.shape
    t_lat = ql.shape[1]
    t_all = t_ctx + t_lat
    n_chunks = t_all // GDN_CHUNK
    nh = B_HEADS
    pad = jnp.zeros((2 * nh,), f32)
    alc = jnp.concatenate([a_log.reshape(-1), pad]).reshape(1, 4 * nh)
    dtc = jnp.concatenate([dt_bias.reshape(-1), pad]).reshape(1, 4 * nh)
    alr = jnp.broadcast_to(a_log.reshape(2 * nh, 1), (2 * nh, GDN_CHUNK))
    dtr = jnp.broadcast_to(dt_bias.reshape(2 * nh, 1), (2 * nh, GDN_CHUNK))
    seg = lambda t, wd: pl.BlockSpec((None, t, wd), lambda b: (b, 0, 0))
    segt = lambda t: pl.BlockSpec((None, 4 * nh, t), lambda b: (b, 0, 0))
    one = lambda shape: pl.BlockSpec(shape, lambda b: (0,) * len(shape))
    in_specs = ([seg(t_ctx, B_WIDTH)] * 4 + [seg(t_ctx, 4 * nh), segt(t_ctx)]
                + [seg(t_lat, B_WIDTH)] * 4 + [seg(t_lat, 4 * nh), segt(t_lat)]
                + [one((1, 4 * nh)), one((1, 4 * nh)), one((2 * nh, GDN_CHUNK)),
                   one((2 * nh, GDN_CHUNK)), one((1, B_HEAD_DIM))])
    return pl.pallas_call(
        functools.partial(_gdn_kernel, t_ctx=t_ctx, t_lat=t_lat),
        out_shape=(jax.ShapeDtypeStruct((n_batch, t_ctx, B_WIDTH), bf16),
                   jax.ShapeDtypeStruct((n_batch, t_lat, B_WIDTH), bf16)),
        grid=(n_batch,),
        in_specs=in_specs,
        out_specs=(seg(t_ctx, B_WIDTH), seg(t_lat, B_WIDTH)),
        scratch_shapes=[
            pltpu.VMEM((t_all, B_WIDTH), bf16), pltpu.VMEM((t_all, B_WIDTH), bf16),
            pltpu.VMEM((t_all, B_WIDTH), bf16), pltpu.VMEM((t_all, B_WIDTH), f32),
            pltpu.VMEM((2 * nh, B_HEAD_DIM, B_HEAD_DIM), f32),
            pltpu.VMEM((n_chunks, GDN_CHUNK, 4 * nh), f32), pltpu.VMEM((n_chunks, GDN_CHUNK, 4 * nh), f32),
            pltpu.VMEM((n_chunks, GDN_CHUNK, 4 * nh), f32),
            pltpu.VMEM((n_chunks, 2 * nh, GDN_CHUNK), f32), pltpu.VMEM((n_chunks, 2 * nh, GDN_CHUNK), f32)],
        compiler_params=_params(1),
        name="gdn",
    )(qc, kc, vc, gc, abc, abtc, ql, kl, vl, gl, abl, abtl, alc, dtc, alr, dtr, norm_g)


def _mix0_kernel(x_ref, auav_ref, yb_ref, lng_ref, lnb_ref, ws_ref, bst_ref, wout_ref, g1_ref,
                 o_ref, mix_ref, *, tm):
    au = auav_ref[:, 0:A_WIDTH].astype(f32)
    av = auav_ref[:, A_WIDTH:2 * A_WIDTH].astype(f32)
    u = _gelu(au)
    gv = _gelu(av)
    mu = jnp.mean(gv, axis=-1, keepdims=True)
    dv = gv - mu
    var = jnp.mean(dv * dv, axis=-1, keepdims=True)
    vn = (dv * lax.rsqrt(var + EPS) * lng_ref[...] + lnb_ref[...]).astype(bf16)
    for c in range(tm // A_CHUNK):
        rows = slice(c * A_CHUNK, (c + 1) * A_CHUNK)
        for g in range(A_GROUPS):
            cols = slice(g * A_GROUP_DIM, (g + 1) * A_GROUP_DIM)
            s = _dot(ws_ref[g], vn[rows, cols]) + bst_ref[:, g:g + 1]
            mix_ref[rows, cols] = (u[rows, cols] * s).astype(bf16)
    mix_ref[:, A_WIDTH:] = yb_ref[...]
    o_ref[...] = x_ref[...] + g1_ref[...] * _dot(mix_ref[...], wout_ref[...])


def _mix0(x, auav, yb, mods, w, is_ctx, tm):
    n_batch, t_len, _ = x.shape
    nb = mods.shape[0] - 1
    return pl.pallas_call(
        functools.partial(_mix0_kernel, tm=tm),
        out_shape=jax.ShapeDtypeStruct(x.shape, f32),
        grid=(n_batch, t_len // tm),
        in_specs=[_tok_spec(tm, D_MODEL), _tok_spec(tm, 2 * A_WIDTH), _tok_spec(tm, B_WIDTH),
                  _full((1, A_WIDTH)), _full((1, A_WIDTH)), _full(w["ws"].shape), _full(w["bst"].shape),
                  _full(w["wout"].shape), _mod_spec(2, is_ctx, nb)],
        out_specs=_tok_spec(tm, D_MODEL),
        scratch_shapes=[pltpu.VMEM((tm, A_WIDTH + B_WIDTH), bf16)],
        compiler_params=_params(2),
        name="mix0_ctx" if is_ctx else "mix0_lat",
    )(x, auav, yb, w["lng"], w["lnb"], w["ws"], w["bst"], w["wout"], mods)


def _ffn_kernel(x_ref, xp_ref, xn_ref, g_ref, sh_ref, sc_ref, gt_ref, wup_ref, conv_ref, wdn_ref, fg_ref,
                o_ref, hs_ref, zs_ref, act_ref, *, tm, final_norm):
    _fill_halo_h(x_ref, xp_ref, xn_ref, g_ref, sh_ref, sc_ref, hs_ref, tm)
    hb_all = hs_ref[...].astype(bf16)
    n = D_FF // FFN_CW
    gcols = lambda j: slice(j * FFN_CW, (j + 1) * FFN_CW)
    ucols = lambda j: slice(D_FF + j * FFN_CW, D_FF + (j + 1) * FFN_CW)

    def up(j):
        zs_ref[j % 2, 0] = _dot(hb_all, wup_ref[:, gcols(j)])
        zs_ref[j % 2, 1] = _dot(hb_all, wup_ref[:, ucols(j)])

    def act(j):
        g = _conv3(zs_ref.at[j % 2, 0], conv_ref[:, gcols(j)], tm)
        u = _conv3(zs_ref.at[j % 2, 1], conv_ref[:, ucols(j)], tm)
        act_ref[:, gcols(j)] = (_silu(g) * u).astype(bf16)

    up(0)
    for j in range(n):
        if j + 1 < n:
            up(j + 1)
        act(j)
    out = x_ref[...] + gt_ref[...] * _dot(act_ref[...], wdn_ref[...])
    if final_norm:
        out = _rms(out, fg_ref[...])
    o_ref[...] = out


def _ffn(x, mods, norm_g, w, final_g, is_ctx, tm, final_norm):
    n_batch, t_len, _ = x.shape
    nb = mods.shape[0] - 1
    return pl.pallas_call(
        functools.partial(_ffn_kernel, tm=tm, final_norm=final_norm),
        out_shape=jax.ShapeDtypeStruct(x.shape, f32),
        grid=(n_batch, t_len // tm),
        in_specs=_halo_specs(tm, t_len) + [
            _full((1, D_MODEL)), _mod_spec(3, is_ctx, nb), _mod_spec(4, is_ctx, nb), _mod_spec(5, is_ctx, nb),
            _full(w["wup"].shape), _full(w["conv"].shape), _full(w["wdn"].shape), _full((1, D_MODEL))],
        out_specs=_tok_spec(tm, D_MODEL),
        scratch_shapes=[pltpu.VMEM((tm + 2 * HALO, D_MODEL), f32),
                        pltpu.VMEM((2, 2, tm + 2 * HALO, FFN_CW), f32),
                        pltpu.VMEM((tm, D_FF), bf16)],
        compiler_params=_params(2),
        name="ffn_ctx" if is_ctx else "ffn_lat",
    )(x, x, x, norm_g, mods, mods, mods, w["wup"], w["conv"], w["wdn"], final_g)


def _mla_proj_kernel(*refs, with_q):
    if with_q:
        (x_ref, g_ref, sh_ref, sc_ref, win_ref, qng_ref, kvng_ref, wkn_ref, wv_ref,
         wqn_ref, wqr_ref, wqrs_ref, cos_ref, sin_ref,
         kn_ref, v_ref, kr_ref, qn_ref, qr_ref) = refs
    else:
        (x_ref, g_ref, sh_ref, sc_ref, win_ref, kvng_ref, wkn_ref, wv_ref,
         kn_ref, v_ref, kr_ref) = refs
    hb = _norm_mod(x_ref[...], g_ref[...], sh_ref[...], sc_ref[...]).astype(bf16)
    z = _dot(hb, win_ref[...])
    off = C_Q_LORA if with_q else 0
    ckvn = _rms(z[:, off:off + C_KV_LORA], kvng_ref[...]).astype(bf16)
    kn_ref[...] = _dot(ckvn, wkn_ref[...]).astype(bf16)
    v_ref[...] = _dot(ckvn, wv_ref[...]).astype(bf16)
    kr = z[:, off + C_KV_LORA:off + C_KV_LORA + C_ROPE]
    if not with_q:
        kr_ref[...] = kr.astype(bf16)
        return
    krs = z[:, off + C_KV_LORA + C_ROPE:off + C_KV_LORA + 2 * C_ROPE]
    cos2, sin2 = cos_ref[...], sin_ref[...]
    kr_ref[...] = (kr * cos2[:, 0:C_ROPE] + krs * sin2[:, 0:C_ROPE]).astype(bf16)
    cqn = _rms(z[:, 0:C_Q_LORA], qng_ref[...]).astype(bf16)
    qn_ref[...] = _dot(cqn, wqn_ref[...]).astype(bf16)
    qr = _dot(cqn, wqr_ref[...])
    qrs = _dot(cqn, wqrs_ref[...])
    for hp in range(C_HEADS // 2):
        cols = slice(hp * 2 * C_ROPE, (hp + 1) * 2 * C_ROPE)
        rot = (qr[:, cols] * cos2 + qrs[:, cols] * sin2).astype(bf16)
        qr_ref[2 * hp] = rot[:, 0:C_ROPE]
        qr_ref[2 * hp + 1] = rot[:, C_ROPE:]


def _mla_proj(x, mods, norm_g, w, with_q, tm):
    n_batch, t_len, _ = x.shape
    nb = mods.shape[0] - 1
    is_ctx = not with_q
    hw = C_HEADS * C_NOPE
    tok = lambda width: jax.ShapeDtypeStruct((n_batch, t_len, width), bf16)
    out_shape = [tok(hw), tok(C_HEADS * C_VDIM), tok(C_ROPE)]
    out_specs = [_tok_spec(tm, hw), _tok_spec(tm, C_HEADS * C_VDIM), _tok_spec(tm, C_ROPE)]
    in_specs = [_tok_spec(tm, D_MODEL), _full((1, D_MODEL)), _mod_spec(0, is_ctx, nb), _mod_spec(1, is_ctx, nb)]
    if with_q:
        args = [w["win"], w["qng"], w["kvng"], w["wkn"], w["wv"], w["wqn"], w["wqr"], w["wqrs"]]
        in_specs += [_full(a.shape) for a in args]
        in_specs += [pl.BlockSpec((tm, 2 * C_ROPE), lambda b, i: (i, 0))] * 2
        args += [w["cos2"], w["sin2"]]
        out_shape += [tok(hw), jax.ShapeDtypeStruct((n_batch, C_HEADS, t_len, C_ROPE), bf16)]
        out_specs += [_tok_spec(tm, hw), pl.BlockSpec((None, C_HEADS, tm, C_ROPE), lambda b, i: (b, 0, i, 0))]
    else:
        args = [w["win_kv"], w["kvng"], w["wkn"], w["wv"]]
        in_specs += [_full(a.shape) for a in args]
    return pl.pallas_call(
        functools.partial(_mla_proj_kernel, with_q=with_q),
        out_shape=tuple(out_shape),
        grid=(n_batch, t_len // tm),
        in_specs=in_specs,
        out_specs=tuple(out_specs),
        compiler_params=_params(2),
        name="mla_proj_lat" if with_q else "mla_proj_ctx",
    )(x, norm_g, mods, mods, *args)


def _attn_kernel(qn_ref, qr_ref, knc_ref, krc_ref, vc_ref, knl_ref, krl_ref, vl_ref, o_ref,
                 kc_s, kl_s, q_s, *, tq):
    pad = C_NOPE - C_ROPE
    for dst, n_ref, r_ref in ((kc_s, knc_ref, krc_ref), (kl_s, knl_ref, krl_ref), (q_s, qn_ref, qr_ref)):
        dst[:, 0:C_NOPE] = n_ref[...]
        dst[:, C_NOPE:C_NOPE + C_ROPE] = r_ref[...]
        dst[:, C_NOPE + C_ROPE:] = jnp.zeros((dst.shape[0], pad), bf16)
    n_tiles = q_s.shape[0] // tq

    def scores(i):
        q = q_s[pl.ds(i * tq, tq), :]
        return (lax.dot_general(q, kc_s[...], NT, preferred_element_type=f32),
                lax.dot_general(q, kl_s[...], NT, preferred_element_type=f32))

    def finish(i, sc, sl):
        m = jnp.maximum(jnp.max(sc, axis=-1, keepdims=True), jnp.max(sl, axis=-1, keepdims=True))
        pc = jnp.exp2(sc - m)
        pl_ = jnp.exp2(sl - m)
        denom = jnp.sum(pc, axis=-1, keepdims=True) + jnp.sum(pl_, axis=-1, keepdims=True)
        o = _dot(pc.astype(bf16), vc_ref[...]) + _dot(pl_.astype(bf16), vl_ref[...])
        o_ref[pl.ds(i * tq, tq), :] = (o / denom).astype(bf16)

    nxt = scores(0)
    for i in range(n_tiles):
        cur = nxt
        if i + 1 < n_tiles:
            nxt = scores(i + 1)
        finish(i, *cur)


def _attention(qn, qr, kn_c, kr_c, v_c, kn_l, kr_l, v_l, tq):
    n_batch, t_lat, _ = qn.shape
    t_ctx = kn_c.shape[1]
    head = lambda t: pl.BlockSpec((None, t, C_NOPE), lambda b, h: (b, 0, h))
    rope = lambda t: pl.BlockSpec((None, t, C_ROPE), lambda b, h: (b, 0, 0))
    return pl.pallas_call(
        functools.partial(_attn_kernel, tq=tq),
        out_shape=jax.ShapeDtypeStruct((n_batch, t_lat, C_HEADS * C_VDIM), bf16),
        grid=(n_batch, C_HEADS),
        in_specs=[head(t_lat), pl.BlockSpec((None, None, t_lat, C_ROPE), lambda b, h: (b, h, 0, 0)),
                  head(t_ctx), rope(t_ctx), head(t_ctx), head(t_lat), rope(t_lat), head(t_lat)],
        out_specs=head(t_lat),
        scratch_shapes=[pltpu.VMEM((t_ctx, 2 * C_NOPE), bf16), pltpu.VMEM((t_lat, 2 * C_NOPE), bf16),
                        pltpu.VMEM((t_lat, 2 * C_NOPE), bf16)],
        compiler_params=_params(2),
        name="mla_attention",
    )(qn, qr, kn_c, kr_c, v_c, kn_l, kr_l, v_l)


def _proj_res_kernel(x_ref, y_ref, w_ref, g1_ref, o_ref):
    o_ref[...] = x_ref[...] + g1_ref[...] * _dot(y_ref[...], w_ref[...])


def _proj_res(x, y, w_out, mods, tm):
    n_batch, t_len, _ = x.shape
    nb = mods.shape[0] - 1
    return pl.pallas_call(
        _proj_res_kernel,
        out_shape=jax.ShapeDtypeStruct(x.shape, f32),
        grid=(n_batch, t_len // tm),
        in_specs=[_tok_spec(tm, D_MODEL), _tok_spec(tm, y.shape[-1]), _full(w_out.shape),
                  _mod_spec(2, False, nb)],
        out_specs=_tok_spec(tm, D_MODEL),
        compiler_params=_params(2),
        name="mla_out_proj",
    )(x, y, w_out, mods)


def _rope_tables(n):
    rows = n // GRID_W
    row = jnp.repeat(jnp.arange(rows, dtype=f32), GRID_W)
    col = jnp.tile(jnp.arange(GRID_W, dtype=f32), rows)
    n_freq = C_ROPE // 4
    inv = ROPE_THETA ** (-jnp.arange(n_freq, dtype=f32) / n_freq)
    ang = jnp.concatenate([row[:, None] * inv, col[:, None] * inv], axis=-1)
    cos, sin = jnp.cos(ang), jnp.sin(ang)
    cos64 = jnp.concatenate([cos, cos], axis=-1)
    sin64 = jnp.concatenate([-sin, sin], axis=-1)
    return jnp.tile(cos64, (1, 2)), jnp.tile(sin64, (1, 2))


def _swap_halves(w):
    h = C_ROPE // 2
    return jnp.concatenate([w[..., h:], w[..., :h]], axis=-1)


def _ffn_weights(w_up, conv_w, w_down):
    return {"wup": w_up.astype(bf16), "conv": conv_w, "wdn": w_down.astype(bf16)}


def kernel(x, c, ctx, c_ctx, ada_w, ada_b, norm1_g, norm2_g, ab_w_in, a_ln_g, a_ln_b, a_ws, a_bs, b_conv_w, b_a_log, b_dt_bias, b_norm_g, ab_w_out, mla_w_in, mla_q_norm_g, mla_kv_norm_g, mla_w_uq, mla_w_ukv, mla_w_out, ffn_w_up, ffn_conv_w, ffn_w_down, final_g):
    n_batch, t_lat, d = x.shape
    t_ctx = ctx.shape[1]
    tm_lat, tm_ctx = 512, t_ctx
    tm_wide = 2 * tm_lat

    cc = jnp.concatenate([c, c_ctx[None, :], jnp.zeros((7, d), f32)], axis=0)
    mods_all = _ada_mods(cc, ada_w, ada_b)
    mods = [mods_all[i, :n_batch + 1].reshape(n_batch + 1, 6, 1, d) for i in range(2)]
    row = lambda v: v.reshape(1, -1)

    a2, b4 = 2 * A_WIDTH, 4 * B_WIDTH
    w_in = ab_w_in[0]
    w_ab = w_in[:, a2 + b4:]
    w0 = {
        "wqkv": w_in[:, a2:a2 + 3 * B_WIDTH].astype(bf16),
        "wrest": jnp.concatenate([w_in[:, :a2], w_in[:, a2 + 3 * B_WIDTH:a2 + b4]], axis=1).astype(bf16),
        "wab": jnp.pad(w_ab, ((0, 0), (0, 128 - 4 * B_HEADS))).astype(bf16),
        "wabt": w_ab.T.astype(bf16),
        "conv": b_conv_w[0],
    }
    parts_c = _inproj0(ctx, mods[0], row(norm1_g[0]), w0, True, tm_ctx)
    parts_l = _inproj0(x, mods[0], row(norm1_g[0]), w0, False, tm_wide)
    yb_c, yb_l = _gdn(parts_c[1:], parts_l[1:], b_a_log[0], b_dt_bias[0], row(b_norm_g[0]))
    wm = {"lng": row(a_ln_g[0]), "lnb": row(a_ln_b[0]), "ws": a_ws[0].astype(bf16), "bst": a_bs[0].T,
          "wout": ab_w_out[0].astype(bf16)}
    cx = _mix0(ctx, parts_c[0], yb_c, mods[0], wm, True, tm_ctx)
    lat = _mix0(x, parts_l[0], yb_l, mods[0], wm, False, tm_wide)
    wf = _ffn_weights(ffn_w_up[0], ffn_conv_w[0], ffn_w_down[0])
    cx = _ffn(cx, mods[0], row(norm2_g[0]), wf, row(final_g), True, tm_ctx, False)
    lat = _ffn(lat, mods[0], row(norm2_g[0]), wf, row(final_g), False, tm_lat, False)

    w_in = mla_w_in[0]
    kv0 = C_Q_LORA
    kr0 = C_Q_LORA + C_KV_LORA
    q_fold = (C_NOPE + C_ROPE) ** -0.5 * math.log2(math.e)
    w_uq = (mla_w_uq[0] * q_fold).reshape(C_Q_LORA, C_HEADS, C_NOPE + C_ROPE)
    w_ukv = mla_w_ukv[0].reshape(C_KV_LORA, C_HEADS, C_NOPE + C_VDIM)
    w_qr = w_uq[:, :, C_NOPE:]
    cos2, sin2 = _rope_tables(t_lat)
    w1 = {
        "win": jnp.concatenate([w_in, _swap_halves(w_in[:, kr0:])], axis=1).astype(bf16),
        "win_kv": jnp.pad(w_in[:, kv0:], ((0, 0), (0, C_ROPE))).astype(bf16),
        "qng": row(mla_q_norm_g[0]), "kvng": row(mla_kv_norm_g[0]),
        "wkn": w_ukv[:, :, :C_NOPE].reshape(C_KV_LORA, -1).astype(bf16),
        "wv": w_ukv[:, :, C_NOPE:].reshape(C_KV_LORA, -1).astype(bf16),
        "wqn": w_uq[:, :, :C_NOPE].reshape(C_Q_LORA, -1).astype(bf16),
        "wqr": w_qr.reshape(C_Q_LORA, -1).astype(bf16),
        "wqrs": _swap_halves(w_qr).reshape(C_Q_LORA, -1).astype(bf16),
        "cos2": cos2, "sin2": sin2,
    }
    kn_c, v_c, kr_c = _mla_proj(cx, mods[1], row(norm1_g[1]), w1, False, tm_ctx)
    kn_l, v_l, kr_l, qn, qr = _mla_proj(lat, mods[1], row(norm1_g[1]), w1, True, tm_wide)
    att = _attention(qn, qr, kn_c, kr_c, v_c, kn_l, kr_l, v_l, 256)
    lat = _proj_res(lat, att, mla_w_out[0].astype(bf16), mods[1], tm_wide)
    wf = _ffn_weights(ffn_w_up[1], ffn_conv_w[1], ffn_w_down[1])
    return _ffn(lat, mods[1], row(norm2_g[1]), wf, row(final_g), False, tm_lat, True)
```

```python
import functools
import math

import jax
import jax.numpy as jnp
from jax import lax
from jax.experimental import pallas as pl
from jax.experimental.pallas import tpu as pltpu

f32 = jnp.float32
bf16 = jnp.bfloat16

D_MODEL = 1024
GRID_W = 64
A_GROUPS = 4
A_GROUP_DIM = 128
A_WIDTH = A_GROUPS * A_GROUP_DIM
A_CHUNK = 128
B_HEADS = 4
B_HEAD_DIM = 128
B_WIDTH = B_HEADS * B_HEAD_DIM
C_HEADS = 8
C_NOPE = 128
C_ROPE = 64
C_VDIM = 128
C_Q_LORA = 384
C_KV_LORA = 256
ROPE_THETA = 10000.0
D_FF = 2816
EPS = 1e-6

HALO = 8
GDN_CHUNK = 128
FFN_CW = 256
VMEM_LIMIT = 56 * 1024 * 1024

NT = (((1,), (1,)), ((), ()))


def _dot(a, b):
    return jnp.dot(a, b, preferred_element_type=f32)


def _silu(x):
    return x * jax.nn.sigmoid(x)


def _gelu(x):
    return 0.5 * x * (1.0 + lax.erf(x * (0.5 ** 0.5)))


def _softplus(x):
    return jnp.maximum(x, 0.0) + jnp.log1p(jnp.exp(-jnp.abs(x)))


def _rms(x, g):
    return x * lax.rsqrt(jnp.mean(x * x, axis=-1, keepdims=True) + EPS) * g


def _norm_mod(x, g, shift, scale):
    return _rms(x, g) * (1.0 + scale) + shift


def _split3(x):
    hi = x.astype(bf16)
    r = x - hi.astype(f32)
    mid = r.astype(bf16)
    lo = (r - mid.astype(f32)).astype(bf16)
    return hi, mid, lo


def _params(n_axes):
    return pltpu.CompilerParams(dimension_semantics=("arbitrary",) * n_axes,
                                vmem_limit_bytes=VMEM_LIMIT)


def _full(shape):
    nd = len(shape)
    return pl.BlockSpec(shape, lambda *_: (0,) * nd, pipeline_mode=pl.Buffered(1))


def _mod_spec(chunk, is_ctx, n_batch):
    if is_ctx:
        return pl.BlockSpec((None, None, 1, D_MODEL), lambda b, i: (n_batch, chunk, 0, 0))
    return pl.BlockSpec((None, None, 1, D_MODEL), lambda b, i: (b, chunk, 0, 0))


def _tok_spec(tm, width):
    return pl.BlockSpec((None, tm, width), lambda b, i: (b, i, 0))


def _halo_specs(tm, t_len):
    r = tm // HALO
    last = t_len // HALO - 1
    return [
        pl.BlockSpec((None, tm, D_MODEL), lambda b, i: (b, i, 0)),
        pl.BlockSpec((None, HALO, D_MODEL), lambda b, i: (b, jnp.maximum(i * r - 1, 0), 0)),
        pl.BlockSpec((None, HALO, D_MODEL), lambda b, i: (b, jnp.minimum((i + 1) * r, last), 0)),
    ]


def _fill_halo_h(x_ref, xp_ref, xn_ref, g_ref, sh_ref, sc_ref, hs_ref, tm):
    i = pl.program_id(1)
    nt = pl.num_programs(1)
    g, sh, sc = g_ref[...], sh_ref[...], sc_ref[...]
    hs_ref[pl.ds(HALO, tm), :] = _norm_mod(x_ref[...], g, sh, sc)
    hs_ref[pl.ds(0, HALO), :] = jnp.where(i > 0, _norm_mod(xp_ref[...], g, sh, sc), 0.0)
    hs_ref[pl.ds(HALO + tm, HALO), :] = jnp.where(i < nt - 1, _norm_mod(xn_ref[...], g, sh, sc), 0.0)


def _conv3(z_ref, cw, tm):
    return (cw[0:1] * z_ref[pl.ds(HALO - 1, tm), :] + cw[1:2] * z_ref[pl.ds(HALO, tm), :]
            + cw[2:3] * z_ref[pl.ds(HALO + 1, tm), :])


def _ada_kernel(c_ref, w_ref, b_ref, o_ref):
    s = _silu(c_ref[...]).astype(bf16)
    o_ref[...] = _dot(s, w_ref[...].astype(bf16)) + b_ref[...]


def _ada_mods(cc, ada_w, ada_b):
    depth = ada_w.shape[0]
    rows = cc.shape[0]
    return pl.pallas_call(
        _ada_kernel,
        out_shape=jax.ShapeDtypeStruct((depth, rows, 6 * D_MODEL), f32),
        grid=(depth, 6),
        in_specs=[pl.BlockSpec((rows, D_MODEL), lambda l, j: (0, 0)),
                  pl.BlockSpec((None, D_MODEL, D_MODEL), lambda l, j: (l, 0, j)),
                  pl.BlockSpec((None, 1, D_MODEL), lambda l, j: (l, 0, j))],
        out_specs=pl.BlockSpec((None, rows, D_MODEL), lambda l, j: (l, 0, j)),
        compiler_params=_params(2),
        name="ada_mods",
    )(cc, ada_w, ada_b.reshape(depth, 1, 6 * D_MODEL))


def _inproj0_kernel(x_ref, xp_ref, xn_ref, g_ref, sh_ref, sc_ref, wqkv_ref, wrest_ref, wab_ref,
                    wabt_ref, conv_ref, auav_ref, gate_ref, q_ref, k_ref, v_ref, ab_ref, abt_ref,
                    hs_ref, zs_ref, *, tm):
    _fill_halo_h(x_ref, xp_ref, xn_ref, g_ref, sh_ref, sc_ref, hs_ref, tm)
    hb_all = hs_ref[...].astype(bf16)
    hb = hs_ref[pl.ds(HALO, tm), :].astype(bf16)
    cols = lambda j: slice(j * B_WIDTH, (j + 1) * B_WIDTH)

    def proj(j):
        zs_ref[j % 2] = _dot(hb_all, wqkv_ref[:, cols(j)])

    def finish(j, o_ref):
        y = _silu(_conv3(zs_ref.at[j % 2], conv_ref[:, cols(j)], tm))
        if j == 2:
            o_ref[...] = y.astype(bf16)
            return
        post = B_HEAD_DIM ** -0.5 if j == 0 else 1.0
        for h in range(B_HEADS):
            yh = y[:, h * B_HEAD_DIM:(h + 1) * B_HEAD_DIM]
            inv = lax.rsqrt(jnp.sum(yh * yh, axis=-1, keepdims=True) + EPS) * post
            o_ref[:, h * B_HEAD_DIM:(h + 1) * B_HEAD_DIM] = (yh * inv).astype(bf16)

    proj(0)
    for j, o_ref in enumerate((q_ref, k_ref, v_ref)):
        if j < 2:
            proj(j + 1)
        finish(j, o_ref)
    auav_ref[:, 0:A_WIDTH] = _dot(hb, wrest_ref[:, 0:A_WIDTH]).astype(bf16)
    auav_ref[:, A_WIDTH:2 * A_WIDTH] = _dot(hb, wrest_ref[:, A_WIDTH:2 * A_WIDTH]).astype(bf16)
    gate_ref[...] = _dot(hb, wrest_ref[:, 2 * A_WIDTH:]).astype(bf16)
    ab_ref[...] = _dot(hb, wab_ref[...])[:, 0:4 * B_HEADS]
    abt_ref[...] = lax.dot_general(wabt_ref[...], hb, NT, preferred_element_type=f32)


def _inproj0(x, mods, norm_g, w, is_ctx, tm):
    n_batch, t_len, _ = x.shape
    nb = mods.shape[0] - 1
    tok = lambda width, dt: jax.ShapeDtypeStruct((n_batch, t_len, width), dt)
    return pl.pallas_call(
        functools.partial(_inproj0_kernel, tm=tm),
        out_shape=(tok(2 * A_WIDTH, bf16), tok(B_WIDTH, bf16), tok(B_WIDTH, bf16), tok(B_WIDTH, bf16),
                   tok(B_WIDTH, bf16), tok(4 * B_HEADS, f32),
                   jax.ShapeDtypeStruct((n_batch, 4 * B_HEADS, t_len), f32)),
        grid=(n_batch, t_len // tm),
        in_specs=_halo_specs(tm, t_len) + [
            _full((1, D_MODEL)), _mod_spec(0, is_ctx, nb), _mod_spec(1, is_ctx, nb),
            _full(w["wqkv"].shape), _full(w["wrest"].shape), _full(w["wab"].shape),
            _full(w["wabt"].shape), _full(w["conv"].shape)],
        out_specs=(_tok_spec(tm, 2 * A_WIDTH), _tok_spec(tm, B_WIDTH), _tok_spec(tm, B_WIDTH),
                   _tok_spec(tm, B_WIDTH), _tok_spec(tm, B_WIDTH), _tok_spec(tm, 4 * B_HEADS),
                   pl.BlockSpec((None, 4 * B_HEADS, tm), lambda b, i: (b, 0, i))),
        scratch_shapes=[pltpu.VMEM((tm + 2 * HALO, D_MODEL), f32),
                        pltpu.VMEM((2, tm + 2 * HALO, B_WIDTH), f32)],
        compiler_params=_params(2),
        name="inproj0_ctx" if is_ctx else "inproj0_lat",
    )(x, x, x, norm_g, mods, mods, w["wqkv"], w["wrest"], w["wab"], w["wabt"], w["conv"])


def _gdn_kernel(qc_ref, kc_ref, vc_ref, gc_ref, abc_ref, abtc_ref,
                ql_ref, kl_ref, vl_ref, gl_ref, abl_ref, abtl_ref,
                alc_ref, dtc_ref, alr_ref, dtr_ref, ng_ref,
                yc_ref, yl_ref,
                q_s, k_s, v_s, o_s, st_s, gcol_s, bcol_s, gtcol_s, grow_s, gtrow_s,
                *, t_ctx, t_lat):
    C = GDN_CHUNK
    n_ctx, n_lat = t_ctx // C, t_lat // C
    n_chunks = n_ctx + n_lat
    nh = B_HEADS

    q_s[pl.ds(0, t_ctx), :] = qc_ref[...]
    q_s[pl.ds(t_ctx, t_lat), :] = ql_ref[...]
    k_s[pl.ds(0, t_ctx), :] = kc_ref[...]
    k_s[pl.ds(t_ctx, t_lat), :] = kl_ref[...]
    v_s[pl.ds(0, t_ctx), :] = vc_ref[...]
    v_s[pl.ds(t_ctx, t_lat), :] = vl_ref[...]
    o_s[...] = jnp.zeros_like(o_s)
    st_s[...] = jnp.zeros_like(st_s)

    ri = lax.broadcasted_iota(jnp.int32, (C, C), 0)
    ci = lax.broadcasted_iota(jnp.int32, (C, C), 1)
    lower = (ri >= ci).astype(bf16)
    upper = (ri <= ci).astype(bf16)
    ones = jnp.ones((C, C), bf16)
    eye = (ri == ci).astype(f32)
    incl = (ri >= ci, ri <= ci)
    strict = (ri > ci, ri < ci)
    pair = ([], [])
    blk = 1
    while blk < C:
        same = (ri // (2 * blk)) == (ci // (2 * blk))
        hi_r, hi_c = (ri % (2 * blk)) >= blk, (ci % (2 * blk)) >= blk
        pair[0].append(same & hi_r & jnp.logical_not(hi_c))
        pair[1].append(same & hi_c & jnp.logical_not(hi_r))
        blk *= 2

    col_is_fwd = lax.broadcasted_iota(jnp.int32, (C, 4 * nh), 1) < nh
    row_is_fwd = lax.broadcasted_iota(jnp.int32, (2 * nh, C), 0) < nh
    for t in range(n_chunks):
        if t < n_ctx:
            ab = abc_ref[pl.ds(t * C, C), :]
            abt = abtc_ref[:, t * C:(t + 1) * C]
        else:
            ab = abl_ref[pl.ds((t - n_ctx) * C, C), :]
            abt = abtl_ref[:, (t - n_ctx) * C:(t - n_ctx + 1) * C]
        la = -jnp.exp(alc_ref[...]) * _softplus(ab + dtc_ref[...])
        p3 = _split3(la)
        pre = sum(_dot(lower, p) for p in p3)
        suf = sum(_dot(upper, p) for p in p3)
        gcol_s[t] = jnp.where(col_is_fwd, pre, suf)
        gtcol_s[t] = sum(_dot(ones, p) for p in p3)
        bcol_s[t] = jax.nn.sigmoid(ab)
        lar = -jnp.exp(alr_ref[...]) * _softplus(abt[0:2 * nh, :] + dtr_ref[...])
        r3 = _split3(lar)
        pre_r = sum(_dot(p, upper) for p in r3)
        suf_r = sum(_dot(p, lower) for p in r3)
        grow_s[t] = jnp.where(row_is_fwd, pre_r, suf_r)
        gtrow_s[t] = sum(_dot(p, ones) for p in r3)

    def chunk_load(t, d, h):
        col = d * nh + h
        r0 = pl.multiple_of(t * C, C)
        hs = slice(h * B_HEAD_DIM, (h + 1) * B_HEAD_DIM)
        return dict(
            q=q_s[pl.ds(r0, C), hs], k=k_s[pl.ds(r0, C), hs], v=v_s[pl.ds(r0, C), hs],
            gc=gcol_s[t][:, col:col + 1], gtc=gtcol_s[t][:, col:col + 1],
            bc=bcol_s[t][:, 2 * nh + col:2 * nh + col + 1],
            gr=grow_s[t][col:col + 1, :], gtr=gtrow_s[t][col:col + 1, :])

    def block_rows(s, d):
        return [(2 * m + 1 - d) * s for m in range(C // (2 * s))]

    def chunk_intra(x, d):
        q, k, v, gc, gtc, bc, gr, gtr = (x[n] for n in ("q", "k", "v", "gc", "gtc", "bc", "gr", "gtr"))
        qk_kk = lax.dot_general(jnp.concatenate([q, k], axis=0), k, NT, preferred_element_type=f32)
        yield
        qk, kk = qk_kk[0:C], qk_kk[C:2 * C]
        decay = jnp.where(incl[d], jnp.exp(jnp.where(incl[d], gc - gr, 0.0)), 0.0)
        a = jnp.where(strict[d], bc * kk * decay, 0.0)
        tinv = eye - jnp.where(pair[d][0], a, 0.0)
        for lvl in range(1, len(pair[d])):
            s = 2 ** lvl
            a_off = jnp.where(pair[d][lvl], a, 0.0).astype(bf16)
            tb16 = tinv.astype(bf16)
            if s < 8:
                ta = _dot(tb16, a_off).astype(bf16)
                yield
                tinv = tinv - _dot(ta, tb16)
                yield
                continue
            offs = block_rows(s, d)
            t_sel = jnp.concatenate([tinv[o:o + s] for o in offs], axis=0)
            ta = _dot(t_sel.astype(bf16), a_off).astype(bf16)
            yield
            t_sel = t_sel - _dot(ta, tb16)
            yield
            pieces, at = [], 0
            for m, o in enumerate(offs):
                pieces += [tinv[at:o], t_sel[m * s:(m + 1) * s]]
                at = o + s
            tinv = jnp.concatenate([p for p in pieces + [tinv[at:C]] if p.shape[0]], axis=0)
        e_gc = jnp.exp(gc)
        qf, kf, vf = q.astype(f32), k.astype(f32), v.astype(f32)
        vb_kbg = jnp.concatenate([vf * bc, kf * (bc * e_gc)], axis=1).astype(bf16)
        uw = _dot(tinv.astype(bf16), vb_kbg)
        yield
        yield dict(u=uw[:, 0:B_HEAD_DIM],
                   wq=jnp.concatenate([uw[:, B_HEAD_DIM:], qf * e_gc], axis=0).astype(bf16),
                   attn=(qk * decay).astype(bf16),
                   kdec_t=(kf.T * jnp.exp(gtr - gr)).astype(bf16),
                   gend=jnp.exp(gtc[0:1, :]))

    def chunk_state(p, s):
        wq_s = _dot(p["wq"], s.astype(bf16))
        yield
        v_new = (p["u"] - wq_s[0:C]).astype(bf16)
        yield s * p["gend"] + _dot(p["kdec_t"], v_new), wq_s[C:2 * C] + _dot(p["attn"], v_new)

    def round_robin(gens):
        while True:
            stage = [next(g) for g in gens]
            if stage[0] is not None:
                return stage

    steps = 2

    def body(j, carry):
        systems = []
        for st in range(steps):
            i = j * steps + st
            tb = jnp.where(i < n_ctx, n_ctx - 1 - i, n_chunks + n_ctx - 1 - i)
            systems += [(t, d, h) for h in range(nh) for d, t in ((0, i), (1, tb))]
        prods = round_robin([chunk_intra(chunk_load(t, d, h), d) for t, d, h in systems])
        state = [st_s[c] for c in range(2 * nh)]
        for st in range(steps):
            group = list(zip(systems, prods))[st * 2 * nh:(st + 1) * 2 * nh]
            outs = round_robin([chunk_state(p, state[d * nh + h]) for (_, d, h), p in group])
            for ((t, d, h), _), (s_new, o_add) in zip(group, outs):
                state[d * nh + h] = s_new
                o_s[pl.ds(pl.multiple_of(t * C, C), C), h * B_HEAD_DIM:(h + 1) * B_HEAD_DIM] += o_add
        for c in range(2 * nh):
            st_s[c] = state[c]
        return carry

    assert n_chunks % steps == 0
    lax.fori_loop(0, n_chunks // steps, body, 0)

    ng = ng_ref[...]
    for t in range(n_chunks):
        if t < n_ctx:
            gate, y_ref, rows = gc_ref[pl.ds(t * C, C), :], yc_ref, pl.ds(t * C, C)
        else:
            gate, y_ref, rows = gl_ref[pl.ds((t - n_ctx) * C, C), :], yl_ref, pl.ds((t - n_ctx) * C, C)
        o = o_s[pl.ds(t * C, C), :]
        gf = gate.astype(f32)
        for h in range(nh):
            hs = slice(h * B_HEAD_DIM, (h + 1) * B_HEAD_DIM)
            y_ref[rows, hs] = (_rms(o[:, hs], ng) * _silu(gf[:, hs])).astype(bf16)


def _gdn(ctx_parts, lat_parts, a_log, dt_bias, norm_g):
    gc, qc, kc, vc, abc, abtc = ctx_parts
    gl, ql, kl, vl, abl, abtl = lat_parts
    n_batch, t_ctx, _ = qc.shape
    t_lat = ql.shape[1]
    t_all = t_ctx + t_lat
    n_chunks = t_all // GDN_CHUNK
    nh = B_HEADS
    pad = jnp.zeros((2 * nh,), f32)
    alc = jnp.concatenate([a_log.reshape(-1), pad]).reshape(1, 4 * nh)
    dtc = jnp.concatenate([dt_bias.reshape(-1), pad]).reshape(1, 4 * nh)
    alr = jnp.broadcast_to(a_log.reshape(2 * nh, 1), (2 * nh, GDN_CHUNK))
    dtr = jnp.broadcast_to(dt_bias.reshape(2 * nh, 1), (2 * nh, GDN_CHUNK))
    seg = lambda t, wd: pl.BlockSpec((None, t, wd), lambda b: (b, 0, 0))
    segt = lambda t: pl.BlockSpec((None, 4 * nh, t), lambda b: (b, 0, 0))
    one = lambda shape: pl.BlockSpec(shape, lambda b: (0,) * len(shape))
    in_specs = ([seg(t_ctx, B_WIDTH)] * 4 + [seg(t_ctx, 4 * nh), segt(t_ctx)]
                + [seg(t_lat, B_WIDTH)] * 4 + [seg(t_lat, 4 * nh), segt(t_lat)]
                + [one((1, 4 * nh)), one((1, 4 * nh)), one((2 * nh, GDN_CHUNK)),
                   one((2 * nh, GDN_CHUNK)), one((1, B_HEAD_DIM))])
    return pl.pallas_call(
        functools.partial(_gdn_kernel, t_ctx=t_ctx, t_lat=t_lat),
        out_shape=(jax.ShapeDtypeStruct((n_batch, t_ctx, B_WIDTH), bf16),
                   jax.ShapeDtypeStruct((n_batch, t_lat, B_WIDTH), bf16)),
        grid=(n_batch,),
        in_specs=in_specs,
        out_specs=(seg(t_ctx, B_WIDTH), seg(t_lat, B_WIDTH)),
        scratch_shapes=[
            pltpu.VMEM((t_all, B_WIDTH), bf16), pltpu.VMEM((t_all, B_WIDTH), bf16),
            pltpu.VMEM((t_all, B_WIDTH), bf16), pltpu.VMEM((t_all, B_WIDTH), f32),
            pltpu.VMEM((2 * nh, B_HEAD_DIM, B_HEAD_DIM), f32),
            pltpu.VMEM((n_chunks, GDN_CHUNK, 4 * nh), f32), pltpu.VMEM((n_chunks, GDN_CHUNK, 4 * nh), f32),
            pltpu.VMEM((n_chunks, GDN_CHUNK, 4 * nh), f32),
            pltpu.VMEM((n_chunks, 2 * nh, GDN_CHUNK), f32), pltpu.VMEM((n_chunks, 2 * nh, GDN_CHUNK), f32)],
        compiler_params=_params(1),
        name="gdn",
    )(qc, kc, vc, gc, abc, abtc, ql, kl, vl, gl, abl, abtl, alc, dtc, alr, dtr, norm_g)


def _mix0_kernel(x_ref, auav_ref, yb_ref, lng_ref, lnb_ref, ws_ref, bst_ref, wout_ref, g1_ref,
                 o_ref, mix_ref, *, tm):
    au = auav_ref[:, 0:A_WIDTH].astype(f32)
    av = auav_ref[:, A_WIDTH:2 * A_WIDTH].astype(f32)
    u = _gelu(au)
    gv = _gelu(av)
    mu = jnp.mean(gv, axis=-1, keepdims=True)
    dv = gv - mu
    var = jnp.mean(dv * dv, axis=-1, keepdims=True)
    vn = (dv * lax.rsqrt(var + EPS) * lng_ref[...] + lnb_ref[...]).astype(bf16)
    for c in range(tm // A_CHUNK):
        rows = slice(c * A_CHUNK, (c + 1) * A_CHUNK)
        for g in range(A_GROUPS):
            cols = slice(g * A_GROUP_DIM, (g + 1) * A_GROUP_DIM)
            s = _dot(ws_ref[g], vn[rows, cols]) + bst_ref[:, g:g + 1]
            mix_ref[rows, cols] = (u[rows, cols] * s).astype(bf16)
    mix_ref[:, A_WIDTH:] = yb_ref[...]
    o_ref[...] = x_ref[...] + g1_ref[...] * _dot(mix_ref[...], wout_ref[...])


def _mix0(x, auav, yb, mods, w, is_ctx, tm):
    n_batch, t_len, _ = x.shape
    nb = mods.shape[0] - 1
    return pl.pallas_call(
        functools.partial(_mix0_kernel, tm=tm),
        out_shape=jax.ShapeDtypeStruct(x.shape, f32),
        grid=(n_batch, t_len // tm),
        in_specs=[_tok_spec(tm, D_MODEL), _tok_spec(tm, 2 * A_WIDTH), _tok_spec(tm, B_WIDTH),
                  _full((1, A_WIDTH)), _full((1, A_WIDTH)), _full(w["ws"].shape), _full(w["bst"].shape),
                  _full(w["wout"].shape), _mod_spec(2, is_ctx, nb)],
        out_specs=_tok_spec(tm, D_MODEL),
        scratch_shapes=[pltpu.VMEM((tm, A_WIDTH + B_WIDTH), bf16)],
        compiler_params=_params(2),
        name="mix0_ctx" if is_ctx else "mix0_lat",
    )(x, auav, yb, w["lng"], w["lnb"], w["ws"], w["bst"], w["wout"], mods)


def _ffn_kernel(x_ref, xp_ref, xn_ref, g_ref, sh_ref, sc_ref, gt_ref, wup_ref, conv_ref, wdn_ref, fg_ref,
                o_ref, hs_ref, zs_ref, act_ref, *, tm, final_norm):
    _fill_halo_h(x_ref, xp_ref, xn_ref, g_ref, sh_ref, sc_ref, hs_ref, tm)
    hb_all = hs_ref[...].astype(bf16)
    n = D_FF // FFN_CW
    gcols = lambda j: slice(j * FFN_CW, (j + 1) * FFN_CW)
    ucols = lambda j: slice(D_FF + j * FFN_CW, D_FF + (j + 1) * FFN_CW)

    def up(j):
        zs_ref[j % 2, 0] = _dot(hb_all, wup_ref[:, gcols(j)])
        zs_ref[j % 2, 1] = _dot(hb_all, wup_ref[:, ucols(j)])

    def act(j):
        g = _conv3(zs_ref.at[j % 2, 0], conv_ref[:, gcols(j)], tm)
        u = _conv3(zs_ref.at[j % 2, 1], conv_ref[:, ucols(j)], tm)
        act_ref[:, gcols(j)] = (_silu(g) * u).astype(bf16)

    up(0)
    for j in range(n):
        if j + 1 < n:
            up(j + 1)
        act(j)
    out = x_ref[...] + gt_ref[...] * _dot(act_ref[...], wdn_ref[...])
    if final_norm:
        out = _rms(out, fg_ref[...])
    o_ref[...] = out


def _ffn(x, mods, norm_g, w, final_g, is_ctx, tm, final_norm):
    n_batch, t_len, _ = x.shape
    nb = mods.shape[0] - 1
    return pl.pallas_call(
        functools.partial(_ffn_kernel, tm=tm, final_norm=final_norm),
        out_shape=jax.ShapeDtypeStruct(x.shape, f32),
        grid=(n_batch, t_len // tm),
        in_specs=_halo_specs(tm, t_len) + [
            _full((1, D_MODEL)), _mod_spec(3, is_ctx, nb), _mod_spec(4, is_ctx, nb), _mod_spec(5, is_ctx, nb),
            _full(w["wup"].shape), _full(w["conv"].shape), _full(w["wdn"].shape), _full((1, D_MODEL))],
        out_specs=_tok_spec(tm, D_MODEL),
        scratch_shapes=[pltpu.VMEM((tm + 2 * HALO, D_MODEL), f32),
                        pltpu.VMEM((2, 2, tm + 2 * HALO, FFN_CW), f32),
                        pltpu.VMEM((tm, D_FF), bf16)],
        compiler_params=_params(2),
        name="ffn_ctx" if is_ctx else "ffn_lat",
    )(x, x, x, norm_g, mods, mods, mods, w["wup"], w["conv"], w["wdn"], final_g)


def _mla_proj_kernel(*refs, with_q):
    if with_q:
        (x_ref, g_ref, sh_ref, sc_ref, win_ref, qng_ref, kvng_ref, wkn_ref, wv_ref,
         wqn_ref, wqr_ref, wqrs_ref, cos_ref, sin_ref,
         kn_ref, v_ref, kr_ref, qn_ref, qr_ref) = refs
    else:
        (x_ref, g_ref, sh_ref, sc_ref, win_ref, kvng_ref, wkn_ref, wv_ref,
         kn_ref, v_ref, kr_ref) = refs
    hb = _norm_mod(x_ref[...], g_ref[...], sh_ref[...], sc_ref[...]).astype(bf16)
    z = _dot(hb, win_ref[...])
    off = C_Q_LORA if with_q else 0
    ckvn = _rms(z[:, off:off + C_KV_LORA], kvng_ref[...]).astype(bf16)
    kn_ref[...] = _dot(ckvn, wkn_ref[...]).astype(bf16)
    v_ref[...] = _dot(ckvn, wv_ref[...]).astype(bf16)
    kr = z[:, off + C_KV_LORA:off + C_KV_LORA + C_ROPE]
    if not with_q:
        kr_ref[...] = kr.astype(bf16)
        return
    krs = z[:, off + C_KV_LORA + C_ROPE:off + C_KV_LORA + 2 * C_ROPE]
    cos2, sin2 = cos_ref[...], sin_ref[...]
    kr_ref[...] = (kr * cos2[:, 0:C_ROPE] + krs * sin2[:, 0:C_ROPE]).astype(bf16)
    cqn = _rms(z[:, 0:C_Q_LORA], qng_ref[...]).astype(bf16)
    qn_ref[...] = _dot(cqn, wqn_ref[...]).astype(bf16)
    qr = _dot(cqn, wqr_ref[...])
    qrs = _dot(cqn, wqrs_ref[...])
    for hp in range(C_HEADS // 2):
        cols = slice(hp * 2 * C_ROPE, (hp + 1) * 2 * C_ROPE)
        rot = (qr[:, cols] * cos2 + qrs[:, cols] * sin2).astype(bf16)
        qr_ref[2 * hp] = rot[:, 0:C_ROPE]
        qr_ref[2 * hp + 1] = rot[:, C_ROPE:]


def _mla_proj(x, mods, norm_g, w, with_q, tm):
    n_batch, t_len, _ = x.shape
    nb = mods.shape[0] - 1
    is_ctx = not with_q
    hw = C_HEADS * C_NOPE
    tok = lambda width: jax.ShapeDtypeStruct((n_batch, t_len, width), bf16)
    out_shape = [tok(hw), tok(C_HEADS * C_VDIM), tok(C_ROPE)]
    out_specs = [_tok_spec(tm, hw), _tok_spec(tm, C_HEADS * C_VDIM), _tok_spec(tm, C_ROPE)]
    in_specs = [_tok_spec(tm, D_MODEL), _full((1, D_MODEL)), _mod_spec(0, is_ctx, nb), _mod_spec(1, is_ctx, nb)]
    if with_q:
        args = [w["win"], w["qng"], w["kvng"], w["wkn"], w["wv"], w["wqn"], w["wqr"], w["wqrs"]]
        in_specs += [_full(a.shape) for a in args]
        in_specs += [pl.BlockSpec((tm, 2 * C_ROPE), lambda b, i: (i, 0))] * 2
        args += [w["cos2"], w["sin2"]]
        out_shape += [tok(hw), jax.ShapeDtypeStruct((n_batch, C_HEADS, t_len, C_ROPE), bf16)]
        out_specs += [_tok_spec(tm, hw), pl.BlockSpec((None, C_HEADS, tm, C_ROPE), lambda b, i: (b, 0, i, 0))]
    else:
        args = [w["win_kv"], w["kvng"], w["wkn"], w["wv"]]
        in_specs += [_full(a.shape) for a in args]
    return pl.pallas_call(
        functools.partial(_mla_proj_kernel, with_q=with_q),
        out_shape=tuple(out_shape),
        grid=(n_batch, t_len // tm),
        in_specs=in_specs,
        out_specs=tuple(out_specs),
        compiler_params=_params(2),
        name="mla_proj_lat" if with_q else "mla_proj_ctx",
    )(x, norm_g, mods, mods, *args)


def _attn_kernel(qn_ref, qr_ref, knc_ref, krc_ref, vc_ref, knl_ref, krl_ref, vl_ref, o_ref,
                 kc_s, kl_s, q_s, *, tq, heads):
    pad = C_NOPE - C_ROPE
    for hh in range(heads):
        cols = slice(hh * C_NOPE, (hh + 1) * C_NOPE)
        for dst, nope, rope in ((kc_s, knc_ref[:, cols], krc_ref[...]), (kl_s, knl_ref[:, cols], krl_ref[...]),
                                (q_s, qn_ref[:, cols], qr_ref[hh])):
            dst[hh, :, 0:C_NOPE] = nope
            dst[hh, :, C_NOPE:C_NOPE + C_ROPE] = rope
            dst[hh, :, C_NOPE + C_ROPE:] = jnp.zeros((dst.shape[1], pad), bf16)
    n_tiles = q_s.shape[1] // tq

    def scores(hh, i):
        q = q_s[hh, pl.ds(i * tq, tq), :]
        return (lax.dot_general(q, kc_s[hh], NT, preferred_element_type=f32),
                lax.dot_general(q, kl_s[hh], NT, preferred_element_type=f32))

    def finish(hh, i, sc, sl):
        cols = slice(hh * C_VDIM, (hh + 1) * C_VDIM)
        m = jnp.maximum(jnp.max(sc, axis=-1, keepdims=True), jnp.max(sl, axis=-1, keepdims=True))
        pc = jnp.exp2(sc - m)
        pl_ = jnp.exp2(sl - m)
        denom = jnp.sum(pc, axis=-1, keepdims=True) + jnp.sum(pl_, axis=-1, keepdims=True)
        o = _dot(pc.astype(bf16), vc_ref[:, cols]) + _dot(pl_.astype(bf16), vl_ref[:, cols])
        o_ref[pl.ds(i * tq, tq), cols] = (o / denom).astype(bf16)

    order = [(hh, i) for hh in range(heads) for i in range(n_tiles)]
    nxt = scores(*order[0])
    for n, tile in enumerate(order):
        cur = nxt
        if n + 1 < len(order):
            nxt = scores(*order[n + 1])
        finish(*tile, *cur)


def _attention(qn, qr, kn_c, kr_c, v_c, kn_l, kr_l, v_l, tq, heads):
    n_batch, t_lat, _ = qn.shape
    t_ctx = kn_c.shape[1]
    group = lambda t: pl.BlockSpec((None, t, heads * C_NOPE), lambda b, h: (b, 0, h))
    rope = lambda t: pl.BlockSpec((None, t, C_ROPE), lambda b, h: (b, 0, 0))
    return pl.pallas_call(
        functools.partial(_attn_kernel, tq=tq, heads=heads),
        out_shape=jax.ShapeDtypeStruct((n_batch, t_lat, C_HEADS * C_VDIM), bf16),
        grid=(n_batch, C_HEADS // heads),
        in_specs=[group(t_lat), pl.BlockSpec((None, heads, t_lat, C_ROPE), lambda b, h: (b, h, 0, 0)),
                  group(t_ctx), rope(t_ctx), group(t_ctx), group(t_lat), rope(t_lat), group(t_lat)],
        out_specs=group(t_lat),
        scratch_shapes=[pltpu.VMEM((heads, t_ctx, 2 * C_NOPE), bf16), pltpu.VMEM((heads, t_lat, 2 * C_NOPE), bf16),
                        pltpu.VMEM((heads, t_lat, 2 * C_NOPE), bf16)],
        compiler_params=_params(2),
        name="mla_attention",
    )(qn, qr, kn_c, kr_c, v_c, kn_l, kr_l, v_l)


def _proj_res_kernel(x_ref, y_ref, w_ref, g1_ref, o_ref):
    o_ref[...] = x_ref[...] + g1_ref[...] * _dot(y_ref[...], w_ref[...])


def _proj_res(x, y, w_out, mods, tm):
    n_batch, t_len, _ = x.shape
    nb = mods.shape[0] - 1
    return pl.pallas_call(
        _proj_res_kernel,
        out_shape=jax.ShapeDtypeStruct(x.shape, f32),
        grid=(n_batch, t_len // tm),
        in_specs=[_tok_spec(tm, D_MODEL), _tok_spec(tm, y.shape[-1]), _full(w_out.shape),
                  _mod_spec(2, False, nb)],
        out_specs=_tok_spec(tm, D_MODEL),
        compiler_params=_params(2),
        name="mla_out_proj",
    )(x, y, w_out, mods)


def _rope_tables(n):
    rows = n // GRID_W
    row = jnp.repeat(jnp.arange(rows, dtype=f32), GRID_W)
    col = jnp.tile(jnp.arange(GRID_W, dtype=f32), rows)
    n_freq = C_ROPE // 4
    inv = ROPE_THETA ** (-jnp.arange(n_freq, dtype=f32) / n_freq)
    ang = jnp.concatenate([row[:, None] * inv, col[:, None] * inv], axis=-1)
    cos, sin = jnp.cos(ang), jnp.sin(ang)
    cos64 = jnp.concatenate([cos, cos], axis=-1)
    sin64 = jnp.concatenate([-sin, sin], axis=-1)
    return jnp.tile(cos64, (1, 2)), jnp.tile(sin64, (1, 2))


def _swap_halves(w):
    h = C_ROPE // 2
    return jnp.concatenate([w[..., h:], w[..., :h]], axis=-1)


def _ffn_weights(w_up, conv_w, w_down):
    return {"wup": w_up.astype(bf16), "conv": conv_w, "wdn": w_down.astype(bf16)}


def kernel(x, c, ctx, c_ctx, ada_w, ada_b, norm1_g, norm2_g, ab_w_in, a_ln_g, a_ln_b, a_ws, a_bs, b_conv_w, b_a_log, b_dt_bias, b_norm_g, ab_w_out, mla_w_in, mla_q_norm_g, mla_kv_norm_g, mla_w_uq, mla_w_ukv, mla_w_out, ffn_w_up, ffn_conv_w, ffn_w_down, final_g):
    n_batch, t_lat, d = x.shape
    t_ctx = ctx.shape[1]
    tm_lat, tm_ctx = 512, t_ctx
    tm_wide = 2 * tm_lat

    cc = jnp.concatenate([c, c_ctx[None, :], jnp.zeros((7, d), f32)], axis=0)
    mods_all = _ada_mods(cc, ada_w, ada_b)
    mods = [mods_all[i, :n_batch + 1].reshape(n_batch + 1, 6, 1, d) for i in range(2)]
    row = lambda v: v.reshape(1, -1)

    a2, b4 = 2 * A_WIDTH, 4 * B_WIDTH
    w_in = ab_w_in[0]
    w_ab = w_in[:, a2 + b4:]
    w0 = {
        "wqkv": w_in[:, a2:a2 + 3 * B_WIDTH].astype(bf16),
        "wrest": jnp.concatenate([w_in[:, :a2], w_in[:, a2 + 3 * B_WIDTH:a2 + b4]], axis=1).astype(bf16),
        "wab": jnp.pad(w_ab, ((0, 0), (0, 128 - 4 * B_HEADS))).astype(bf16),
        "wabt": w_ab.T.astype(bf16),
        "conv": b_conv_w[0],
    }
    parts_c = _inproj0(ctx, mods[0], row(norm1_g[0]), w0, True, tm_ctx)
    parts_l = _inproj0(x, mods[0], row(norm1_g[0]), w0, False, tm_wide)
    yb_c, yb_l = _gdn(parts_c[1:], parts_l[1:], b_a_log[0], b_dt_bias[0], row(b_norm_g[0]))
    wm = {"lng": row(a_ln_g[0]), "lnb": row(a_ln_b[0]), "ws": a_ws[0].astype(bf16), "bst": a_bs[0].T,
          "wout": ab_w_out[0].astype(bf16)}
    flat = lambda a: a.reshape(1, n_batch * t_ctx, a.shape[-1])
    unflat = lambda a: a.reshape(n_batch, t_ctx, a.shape[-1])
    tm_flat = math.gcd(tm_wide, n_batch * t_ctx)
    cx = unflat(_mix0(flat(ctx), flat(parts_c[0]), flat(yb_c), mods[0], wm, True, tm_flat))
    lat = _mix0(x, parts_l[0], yb_l, mods[0], wm, False, tm_wide)
    wf = _ffn_weights(ffn_w_up[0], ffn_conv_w[0], ffn_w_down[0])
    cx = _ffn(cx, mods[0], row(norm2_g[0]), wf, row(final_g), True, tm_ctx, False)
    lat = _ffn(lat, mods[0], row(norm2_g[0]), wf, row(final_g), False, tm_lat, False)

    w_in = mla_w_in[0]
    kv0 = C_Q_LORA
    kr0 = C_Q_LORA + C_KV_LORA
    q_fold = (C_NOPE + C_ROPE) ** -0.5 * math.log2(math.e)
    w_uq = (mla_w_uq[0] * q_fold).reshape(C_Q_LORA, C_HEADS, C_NOPE + C_ROPE)
    w_ukv = mla_w_ukv[0].reshape(C_KV_LORA, C_HEADS, C_NOPE + C_VDIM)
    w_qr = w_uq[:, :, C_NOPE:]
    cos2, sin2 = _rope_tables(t_lat)
    w1 = {
        "win": jnp.concatenate([w_in, _swap_halves(w_in[:, kr0:])], axis=1).astype(bf16),
        "win_kv": jnp.pad(w_in[:, kv0:], ((0, 0), (0, C_ROPE))).astype(bf16),
        "qng": row(mla_q_norm_g[0]), "kvng": row(mla_kv_norm_g[0]),
        "wkn": w_ukv[:, :, :C_NOPE].reshape(C_KV_LORA, -1).astype(bf16),
        "wv": w_ukv[:, :, C_NOPE:].reshape(C_KV_LORA, -1).astype(bf16),
        "wqn": w_uq[:, :, :C_NOPE].reshape(C_Q_LORA, -1).astype(bf16),
        "wqr": w_qr.reshape(C_Q_LORA, -1).astype(bf16),
        "wqrs": _swap_halves(w_qr).reshape(C_Q_LORA, -1).astype(bf16),
        "cos2": cos2, "sin2": sin2,
    }
    kn_c, v_c, kr_c = (unflat(a) for a in _mla_proj(flat(cx), mods[1], row(norm1_g[1]), w1, False, tm_flat))
    kn_l, v_l, kr_l, qn, qr = _mla_proj(lat, mods[1], row(norm1_g[1]), w1, True, tm_wide)
    att = _attention(qn, qr, kn_c, kr_c, v_c, kn_l, kr_l, v_l, 256, 2)
    lat = _proj_res(lat, att, mla_w_out[0].astype(bf16), mods[1], tm_wide)
    wf = _ffn_weights(ffn_w_up[1], ffn_conv_w[1], ffn_w_down[1])
    return _ffn(lat, mods[1], row(norm2_g[1]), wf, row(final_g), False, tm_lat, True)
```

```python
import functools
import math

import jax
import jax.numpy as jnp
from jax import lax
from jax.experimental import pallas as pl
from jax.experimental.pallas import tpu as pltpu

f32 = jnp.float32
bf16 = jnp.bfloat16

D_MODEL = 1024
GRID_W = 64
A_GROUPS = 4
A_GROUP_DIM = 128
A_WIDTH = A_GROUPS * A_GROUP_DIM
A_CHUNK = 128
B_HEADS = 4
B_HEAD_DIM = 128
B_WIDTH = B_HEADS * B_HEAD_DIM
C_HEADS = 8
C_NOPE = 128
C_ROPE = 64
C_VDIM = 128
C_Q_LORA = 384
C_KV_LORA = 256
ROPE_THETA = 10000.0
D_FF = 2816
EPS = 1e-6

HALO = 8
GDN_CHUNK = 128
FFN_CW = 256
VMEM_LIMIT = 56 * 1024 * 1024

NT = (((1,), (1,)), ((), ()))


def _dot(a, b):
    return jnp.dot(a, b, preferred_element_type=f32)


def _silu(x):
    return x * jax.nn.sigmoid(x)


def _gelu(x):
    return 0.5 * x * (1.0 + lax.erf(x * (0.5 ** 0.5)))


def _softplus(x):
    return jnp.maximum(x, 0.0) + jnp.log1p(jnp.exp(-jnp.abs(x)))


def _rms(x, g):
    return x * lax.rsqrt(jnp.mean(x * x, axis=-1, keepdims=True) + EPS) * g


def _norm_mod(x, g, shift, scale):
    return _rms(x, g) * (1.0 + scale) + shift


def _split3(x):
    hi = x.astype(bf16)
    r = x - hi.astype(f32)
    mid = r.astype(bf16)
    lo = (r - mid.astype(f32)).astype(bf16)
    return hi, mid, lo


def _params(n_axes):
    return pltpu.CompilerParams(dimension_semantics=("arbitrary",) * n_axes,
                                vmem_limit_bytes=VMEM_LIMIT)


def _full(shape):
    nd = len(shape)
    return pl.BlockSpec(shape, lambda *_: (0,) * nd, pipeline_mode=pl.Buffered(1))


def _mod_spec(chunk, is_ctx, n_batch):
    if is_ctx:
        return pl.BlockSpec((None, None, 1, D_MODEL), lambda b, i: (n_batch, chunk, 0, 0))
    return pl.BlockSpec((None, None, 1, D_MODEL), lambda b, i: (b, chunk, 0, 0))


def _tok_spec(tm, width):
    return pl.BlockSpec((None, tm, width), lambda b, i: (b, i, 0))


def _halo_specs(tm, t_len):
    r = tm // HALO
    last = t_len // HALO - 1
    return [
        pl.BlockSpec((None, tm, D_MODEL), lambda b, i: (b, i, 0)),
        pl.BlockSpec((None, HALO, D_MODEL), lambda b, i: (b, jnp.maximum(i * r - 1, 0), 0)),
        pl.BlockSpec((None, HALO, D_MODEL), lambda b, i: (b, jnp.minimum((i + 1) * r, last), 0)),
    ]


def _fill_halo_h(x_ref, xp_ref, xn_ref, g_ref, sh_ref, sc_ref, hs_ref, tm):
    i = pl.program_id(1)
    nt = pl.num_programs(1)
    g, sh, sc = g_ref[...], sh_ref[...], sc_ref[...]
    hs_ref[pl.ds(HALO, tm), :] = _norm_mod(x_ref[...], g, sh, sc)
    hs_ref[pl.ds(0, HALO), :] = jnp.where(i > 0, _norm_mod(xp_ref[...], g, sh, sc), 0.0)
    hs_ref[pl.ds(HALO + tm, HALO), :] = jnp.where(i < nt - 1, _norm_mod(xn_ref[...], g, sh, sc), 0.0)


def _conv3(z_ref, cw, tm):
    return (cw[0:1] * z_ref[pl.ds(HALO - 1, tm), :] + cw[1:2] * z_ref[pl.ds(HALO, tm), :]
            + cw[2:3] * z_ref[pl.ds(HALO + 1, tm), :])


def _ada_kernel(c_ref, w_ref, b_ref, o_ref):
    s = _silu(c_ref[...]).astype(bf16)
    o_ref[...] = _dot(s, w_ref[...].astype(bf16)) + b_ref[...]


def _ada_mods(cc, ada_w, ada_b):
    depth = ada_w.shape[0]
    rows = cc.shape[0]
    return pl.pallas_call(
        _ada_kernel,
        out_shape=jax.ShapeDtypeStruct((depth, rows, 6 * D_MODEL), f32),
        grid=(depth, 6),
        in_specs=[pl.BlockSpec((rows, D_MODEL), lambda l, j: (0, 0)),
                  pl.BlockSpec((None, D_MODEL, D_MODEL), lambda l, j: (l, 0, j)),
                  pl.BlockSpec((None, 1, D_MODEL), lambda l, j: (l, 0, j))],
        out_specs=pl.BlockSpec((None, rows, D_MODEL), lambda l, j: (l, 0, j)),
        compiler_params=_params(2),
        name="ada_mods",
    )(cc, ada_w, ada_b.reshape(depth, 1, 6 * D_MODEL))


def _inproj0_kernel(x_ref, xp_ref, xn_ref, g_ref, sh_ref, sc_ref, wqkv_ref, wrest_ref, wab_ref,
                    wabt_ref, conv_ref, auav_ref, gate_ref, q_ref, k_ref, v_ref, ab_ref, abt_ref,
                    hs_ref, zs_ref, *, tm):
    _fill_halo_h(x_ref, xp_ref, xn_ref, g_ref, sh_ref, sc_ref, hs_ref, tm)
    hb_all = hs_ref[...].astype(bf16)
    hb = hs_ref[pl.ds(HALO, tm), :].astype(bf16)
    cols = lambda j: slice(j * B_WIDTH, (j + 1) * B_WIDTH)

    def proj(j):
        zs_ref[j % 2] = _dot(hb_all, wqkv_ref[:, cols(j)])

    def finish(j, o_ref):
        y = _silu(_conv3(zs_ref.at[j % 2], conv_ref[:, cols(j)], tm))
        if j == 2:
            o_ref[...] = y.astype(bf16)
            return
        post = B_HEAD_DIM ** -0.5 if j == 0 else 1.0
        for h in range(B_HEADS):
            yh = y[:, h * B_HEAD_DIM:(h + 1) * B_HEAD_DIM]
            inv = lax.rsqrt(jnp.sum(yh * yh, axis=-1, keepdims=True) + EPS) * post
            o_ref[:, h * B_HEAD_DIM:(h + 1) * B_HEAD_DIM] = (yh * inv).astype(bf16)

    proj(0)
    for j, o_ref in enumerate((q_ref, k_ref, v_ref)):
        if j < 2:
            proj(j + 1)
        finish(j, o_ref)
    auav_ref[:, 0:A_WIDTH] = _dot(hb, wrest_ref[:, 0:A_WIDTH]).astype(bf16)
    auav_ref[:, A_WIDTH:2 * A_WIDTH] = _dot(hb, wrest_ref[:, A_WIDTH:2 * A_WIDTH]).astype(bf16)
    gate_ref[...] = _dot(hb, wrest_ref[:, 2 * A_WIDTH:]).astype(bf16)
    ab_ref[...] = _dot(hb, wab_ref[...])[:, 0:4 * B_HEADS]
    abt_ref[...] = lax.dot_general(wabt_ref[...], hb, NT, preferred_element_type=f32)


def _inproj0(x, mods, norm_g, w, is_ctx, tm):
    n_batch, t_len, _ = x.shape
    nb = mods.shape[0] - 1
    tok = lambda width, dt: jax.ShapeDtypeStruct((n_batch, t_len, width), dt)
    return pl.pallas_call(
        functools.partial(_inproj0_kernel, tm=tm),
        out_shape=(tok(2 * A_WIDTH, bf16), tok(B_WIDTH, bf16), tok(B_WIDTH, bf16), tok(B_WIDTH, bf16),
                   tok(B_WIDTH, bf16), tok(4 * B_HEADS, f32),
                   jax.ShapeDtypeStruct((n_batch, 4 * B_HEADS, t_len), f32)),
        grid=(n_batch, t_len // tm),
        in_specs=_halo_specs(tm, t_len) + [
            _full((1, D_MODEL)), _mod_spec(0, is_ctx, nb), _mod_spec(1, is_ctx, nb),
            _full(w["wqkv"].shape), _full(w["wrest"].shape), _full(w["wab"].shape),
            _full(w["wabt"].shape), _full(w["conv"].shape)],
        out_specs=(_tok_spec(tm, 2 * A_WIDTH), _tok_spec(tm, B_WIDTH), _tok_spec(tm, B_WIDTH),
                   _tok_spec(tm, B_WIDTH), _tok_spec(tm, B_WIDTH), _tok_spec(tm, 4 * B_HEADS),
                   pl.BlockSpec((None, 4 * B_HEADS, tm), lambda b, i: (b, 0, i))),
        scratch_shapes=[pltpu.VMEM((tm + 2 * HALO, D_MODEL), f32),
                        pltpu.VMEM((2, tm + 2 * HALO, B_WIDTH), f32)],
        compiler_params=_params(2),
        name="inproj0_ctx" if is_ctx else "inproj0_lat",
    )(x, x, x, norm_g, mods, mods, w["wqkv"], w["wrest"], w["wab"], w["wabt"], w["conv"])


def _gdn_kernel(qc_ref, kc_ref, vc_ref, gc_ref, abc_ref, abtc_ref,
                ql_ref, kl_ref, vl_ref, gl_ref, abl_ref, abtl_ref,
                alc_ref, dtc_ref, alr_ref, dtr_ref, ng_ref,
                yc_ref, yl_ref,
                q_s, k_s, v_s, o_s, st_s, gcol_s, bcol_s, gtcol_s, grow_s, gtrow_s,
                *, t_ctx, t_lat):
    C = GDN_CHUNK
    n_ctx, n_lat = t_ctx // C, t_lat // C
    n_chunks = n_ctx + n_lat
    nh = B_HEADS

    q_s[pl.ds(0, t_ctx), :] = qc_ref[...]
    q_s[pl.ds(t_ctx, t_lat), :] = ql_ref[...]
    k_s[pl.ds(0, t_ctx), :] = kc_ref[...]
    k_s[pl.ds(t_ctx, t_lat), :] = kl_ref[...]
    v_s[pl.ds(0, t_ctx), :] = vc_ref[...]
    v_s[pl.ds(t_ctx, t_lat), :] = vl_ref[...]
    o_s[...] = jnp.zeros_like(o_s)
    st_s[...] = jnp.zeros_like(st_s)

    ri = lax.broadcasted_iota(jnp.int32, (C, C), 0)
    ci = lax.broadcasted_iota(jnp.int32, (C, C), 1)
    lower = (ri >= ci).astype(bf16)
    upper = (ri <= ci).astype(bf16)
    ones = jnp.ones((C, C), bf16)
    eye = (ri == ci).astype(f32)
    incl = (ri >= ci, ri <= ci)
    strict = (ri > ci, ri < ci)
    pair = ([], [])
    blk = 1
    while blk < C:
        same = (ri // (2 * blk)) == (ci // (2 * blk))
        hi_r, hi_c = (ri % (2 * blk)) >= blk, (ci % (2 * blk)) >= blk
        pair[0].append(same & hi_r & jnp.logical_not(hi_c))
        pair[1].append(same & hi_c & jnp.logical_not(hi_r))
        blk *= 2

    col_is_fwd = lax.broadcasted_iota(jnp.int32, (C, 4 * nh), 1) < nh
    row_is_fwd = lax.broadcasted_iota(jnp.int32, (2 * nh, C), 0) < nh
    for t in range(n_chunks):
        if t < n_ctx:
            ab = abc_ref[pl.ds(t * C, C), :]
            abt = abtc_ref[:, t * C:(t + 1) * C]
        else:
            ab = abl_ref[pl.ds((t - n_ctx) * C, C), :]
            abt = abtl_ref[:, (t - n_ctx) * C:(t - n_ctx + 1) * C]
        la = -jnp.exp(alc_ref[...]) * _softplus(ab + dtc_ref[...])
        p3 = _split3(la)
        pre = sum(_dot(lower, p) for p in p3)
        suf = sum(_dot(upper, p) for p in p3)
        gcol_s[t] = jnp.where(col_is_fwd, pre, suf)
        gtcol_s[t] = sum(_dot(ones, p) for p in p3)
        bcol_s[t] = jax.nn.sigmoid(ab)
        lar = -jnp.exp(alr_ref[...]) * _softplus(abt[0:2 * nh, :] + dtr_ref[...])
        r3 = _split3(lar)
        pre_r = sum(_dot(p, upper) for p in r3)
        suf_r = sum(_dot(p, lower) for p in r3)
        grow_s[t] = jnp.where(row_is_fwd, pre_r, suf_r)
        gtrow_s[t] = sum(_dot(p, ones) for p in r3)

    def chunk_load(t, d, h):
        col = d * nh + h
        r0 = pl.multiple_of(t * C, C)
        hs = slice(h * B_HEAD_DIM, (h + 1) * B_HEAD_DIM)
        return dict(
            q=q_s[pl.ds(r0, C), hs], k=k_s[pl.ds(r0, C), hs], v=v_s[pl.ds(r0, C), hs],
            gc=gcol_s[t][:, col:col + 1], gtc=gtcol_s[t][:, col:col + 1],
            bc=bcol_s[t][:, 2 * nh + col:2 * nh + col + 1],
            gr=grow_s[t][col:col + 1, :], gtr=gtrow_s[t][col:col + 1, :])

    def block_rows(s, d):
        return [(2 * m + 1 - d) * s for m in range(C // (2 * s))]

    def chunk_intra(x, d):
        q, k, v, gc, gtc, bc, gr, gtr = (x[n] for n in ("q", "k", "v", "gc", "gtc", "bc", "gr", "gtr"))
        qk_kk = lax.dot_general(jnp.concatenate([q, k], axis=0), k, NT, preferred_element_type=f32)
        yield
        qk, kk = qk_kk[0:C], qk_kk[C:2 * C]
        decay = jnp.where(incl[d], jnp.exp(jnp.where(incl[d], gc - gr, 0.0)), 0.0)
        a = jnp.where(strict[d], bc * kk * decay, 0.0)
        tinv = eye - jnp.where(pair[d][0], a, 0.0)
        for lvl in range(1, len(pair[d])):
            s = 2 ** lvl
            a_off = jnp.where(pair[d][lvl], a, 0.0).astype(bf16)
            tb16 = tinv.astype(bf16)
            if s < 8:
                ta = _dot(tb16, a_off).astype(bf16)
                yield
                tinv = tinv - _dot(ta, tb16)
                yield
                continue
            offs = block_rows(s, d)
            t_sel = jnp.concatenate([tinv[o:o + s] for o in offs], axis=0)
            ta = _dot(t_sel.astype(bf16), a_off).astype(bf16)
            yield
            t_sel = t_sel - _dot(ta, tb16)
            yield
            pieces, at = [], 0
            for m, o in enumerate(offs):
                pieces += [tinv[at:o], t_sel[m * s:(m + 1) * s]]
                at = o + s
            tinv = jnp.concatenate([p for p in pieces + [tinv[at:C]] if p.shape[0]], axis=0)
        e_gc = jnp.exp(gc)
        qf, kf, vf = q.astype(f32), k.astype(f32), v.astype(f32)
        vb_kbg = jnp.concatenate([vf * bc, kf * (bc * e_gc)], axis=1).astype(bf16)
        uw = _dot(tinv.astype(bf16), vb_kbg)
        yield
        yield dict(u=uw[:, 0:B_HEAD_DIM],
                   wq=jnp.concatenate([uw[:, B_HEAD_DIM:], qf * e_gc], axis=0).astype(bf16),
                   attn=(qk * decay).astype(bf16),
                   kdec_t=(kf.T * jnp.exp(gtr - gr)).astype(bf16),
                   gend=jnp.exp(gtc[0:1, :]))

    def chunk_state(p, s):
        wq_s = _dot(p["wq"], s.astype(bf16))
        yield
        v_new = (p["u"] - wq_s[0:C]).astype(bf16)
        yield s * p["gend"] + _dot(p["kdec_t"], v_new), wq_s[C:2 * C] + _dot(p["attn"], v_new)

    def round_robin(gens):
        while True:
            stage = [next(g) for g in gens]
            if stage[0] is not None:
                return stage

    steps = 3

    def body(j, carry):
        systems = []
        for st in range(steps):
            i = j * steps + st
            tb = jnp.where(i < n_ctx, n_ctx - 1 - i, n_chunks + n_ctx - 1 - i)
            systems += [(t, d, h) for h in range(nh) for d, t in ((0, i), (1, tb))]
        prods = round_robin([chunk_intra(chunk_load(t, d, h), d) for t, d, h in systems])
        state = [st_s[c] for c in range(2 * nh)]
        for st in range(steps):
            group = list(zip(systems, prods))[st * 2 * nh:(st + 1) * 2 * nh]
            outs = round_robin([chunk_state(p, state[d * nh + h]) for (_, d, h), p in group])
            for ((t, d, h), _), (s_new, o_add) in zip(group, outs):
                state[d * nh + h] = s_new
                o_s[pl.ds(pl.multiple_of(t * C, C), C), h * B_HEAD_DIM:(h + 1) * B_HEAD_DIM] += o_add
        for c in range(2 * nh):
            st_s[c] = state[c]
        return carry

    assert n_chunks % steps == 0
    lax.fori_loop(0, n_chunks // steps, body, 0)

    ng = ng_ref[...]
    for t in range(n_chunks):
        if t < n_ctx:
            gate, y_ref, rows = gc_ref[pl.ds(t * C, C), :], yc_ref, pl.ds(t * C, C)
        else:
            gate, y_ref, rows = gl_ref[pl.ds((t - n_ctx) * C, C), :], yl_ref, pl.ds((t - n_ctx) * C, C)
        o = o_s[pl.ds(t * C, C), :]
        gf = gate.astype(f32)
        for h in range(nh):
            hs = slice(h * B_HEAD_DIM, (h + 1) * B_HEAD_DIM)
            y_ref[rows, hs] = (_rms(o[:, hs], ng) * _silu(gf[:, hs])).astype(bf16)


def _gdn(ctx_parts, lat_parts, a_log, dt_bias, norm_g):
    gc, qc, kc, vc, abc, abtc = ctx_parts
    gl, ql, kl, vl, abl, abtl = lat_parts
    n_batch, t_ctx, _ = qc.shape
    t_lat = ql.shape[1]
    t_all = t_ctx + t_lat
    n_chunks = t_all // GDN_CHUNK
    nh = B_HEADS
    pad = jnp.zeros((2 * nh,), f32)
    alc = jnp.concatenate([a_log.reshape(-1), pad]).reshape(1, 4 * nh)
    dtc = jnp.concatenate([dt_bias.reshape(-1), pad]).reshape(1, 4 * nh)
    alr = jnp.broadcast_to(a_log.reshape(2 * nh, 1), (2 * nh, GDN_CHUNK))
    dtr = jnp.broadcast_to(dt_bias.reshape(2 * nh, 1), (2 * nh, GDN_CHUNK))
    seg = lambda t, wd: pl.BlockSpec((None, t, wd), lambda b: (b, 0, 0))
    segt = lambda t: pl.BlockSpec((None, 4 * nh, t), lambda b: (b, 0, 0))
    one = lambda shape: pl.BlockSpec(shape, lambda b: (0,) * len(shape))
    in_specs = ([seg(t_ctx, B_WIDTH)] * 4 + [seg(t_ctx, 4 * nh), segt(t_ctx)]
                + [seg(t_lat, B_WIDTH)] * 4 + [seg(t_lat, 4 * nh), segt(t_lat)]
                + [one((1, 4 * nh)), one((1, 4 * nh)), one((2 * nh, GDN_CHUNK)),
                   one((2 * nh, GDN_CHUNK)), one((1, B_HEAD_DIM))])
    return pl.pallas_call(
        functools.partial(_gdn_kernel, t_ctx=t_ctx, t_lat=t_lat),
        out_shape=(jax.ShapeDtypeStruct((n_batch, t_ctx, B_WIDTH), bf16),
                   jax.ShapeDtypeStruct((n_batch, t_lat, B_WIDTH), bf16)),
        grid=(n_batch,),
        in_specs=in_specs,
        out_specs=(seg(t_ctx, B_WIDTH), seg(t_lat, B_WIDTH)),
        scratch_shapes=[
            pltpu.VMEM((t_all, B_WIDTH), bf16), pltpu.VMEM((t_all, B_WIDTH), bf16),
            pltpu.VMEM((t_all, B_WIDTH), bf16), pltpu.VMEM((t_all, B_WIDTH), f32),
            pltpu.VMEM((2 * nh, B_HEAD_DIM, B_HEAD_DIM), f32),
            pltpu.VMEM((n_chunks, GDN_CHUNK, 4 * nh), f32), pltpu.VMEM((n_chunks, GDN_CHUNK, 4 * nh), f32),
            pltpu.VMEM((n_chunks, GDN_CHUNK, 4 * nh), f32),
            pltpu.VMEM((n_chunks, 2 * nh, GDN_CHUNK), f32), pltpu.VMEM((n_chunks, 2 * nh, GDN_CHUNK), f32)],
        compiler_params=_params(1),
        name="gdn",
    )(qc, kc, vc, gc, abc, abtc, ql, kl, vl, gl, abl, abtl, alc, dtc, alr, dtr, norm_g)


def _mix0_kernel(x_ref, auav_ref, yb_ref, lng_ref, lnb_ref, ws_ref, bst_ref, wout_ref, g1_ref,
                 o_ref, mix_ref, *, tm):
    au = auav_ref[:, 0:A_WIDTH].astype(f32)
    av = auav_ref[:, A_WIDTH:2 * A_WIDTH].astype(f32)
    u = _gelu(au)
    gv = _gelu(av)
    mu = jnp.mean(gv, axis=-1, keepdims=True)
    dv = gv - mu
    var = jnp.mean(dv * dv, axis=-1, keepdims=True)
    vn = (dv * lax.rsqrt(var + EPS) * lng_ref[...] + lnb_ref[...]).astype(bf16)
    for c in range(tm // A_CHUNK):
        rows = slice(c * A_CHUNK, (c + 1) * A_CHUNK)
        for g in range(A_GROUPS):
            cols = slice(g * A_GROUP_DIM, (g + 1) * A_GROUP_DIM)
            s = _dot(ws_ref[g], vn[rows, cols]) + bst_ref[:, g:g + 1]
            mix_ref[rows, cols] = (u[rows, cols] * s).astype(bf16)
    mix_ref[:, A_WIDTH:] = yb_ref[...]
    o_ref[...] = x_ref[...] + g1_ref[...] * _dot(mix_ref[...], wout_ref[...])


def _mix0(x, auav, yb, mods, w, is_ctx, tm):
    n_batch, t_len, _ = x.shape
    nb = mods.shape[0] - 1
    return pl.pallas_call(
        functools.partial(_mix0_kernel, tm=tm),
        out_shape=jax.ShapeDtypeStruct(x.shape, f32),
        grid=(n_batch, t_len // tm),
        in_specs=[_tok_spec(tm, D_MODEL), _tok_spec(tm, 2 * A_WIDTH), _tok_spec(tm, B_WIDTH),
                  _full((1, A_WIDTH)), _full((1, A_WIDTH)), _full(w["ws"].shape), _full(w["bst"].shape),
                  _full(w["wout"].shape), _mod_spec(2, is_ctx, nb)],
        out_specs=_tok_spec(tm, D_MODEL),
        scratch_shapes=[pltpu.VMEM((tm, A_WIDTH + B_WIDTH), bf16)],
        compiler_params=_params(2),
        name="mix0_ctx" if is_ctx else "mix0_lat",
    )(x, auav, yb, w["lng"], w["lnb"], w["ws"], w["bst"], w["wout"], mods)


def _ffn_kernel(x_ref, xp_ref, xn_ref, g_ref, sh_ref, sc_ref, gt_ref, wup_ref, conv_ref, wdn_ref, fg_ref,
                o_ref, hs_ref, zs_ref, act_ref, *, tm, final_norm):
    _fill_halo_h(x_ref, xp_ref, xn_ref, g_ref, sh_ref, sc_ref, hs_ref, tm)
    hb_all = hs_ref[...].astype(bf16)
    n = D_FF // FFN_CW
    gcols = lambda j: slice(j * FFN_CW, (j + 1) * FFN_CW)
    ucols = lambda j: slice(D_FF + j * FFN_CW, D_FF + (j + 1) * FFN_CW)

    def up(j):
        zs_ref[j % 2, 0] = _dot(hb_all, wup_ref[:, gcols(j)])
        zs_ref[j % 2, 1] = _dot(hb_all, wup_ref[:, ucols(j)])

    def act(j):
        g = _conv3(zs_ref.at[j % 2, 0], conv_ref[:, gcols(j)], tm)
        u = _conv3(zs_ref.at[j % 2, 1], conv_ref[:, ucols(j)], tm)
        act_ref[:, gcols(j)] = (_silu(g) * u).astype(bf16)

    up(0)
    for j in range(n):
        if j + 1 < n:
            up(j + 1)
        act(j)
    out = x_ref[...] + gt_ref[...] * _dot(act_ref[...], wdn_ref[...])
    if final_norm:
        out = _rms(out, fg_ref[...])
    o_ref[...] = out


def _ffn(x, mods, norm_g, w, final_g, is_ctx, tm, final_norm):
    n_batch, t_len, _ = x.shape
    nb = mods.shape[0] - 1
    return pl.pallas_call(
        functools.partial(_ffn_kernel, tm=tm, final_norm=final_norm),
        out_shape=jax.ShapeDtypeStruct(x.shape, f32),
        grid=(n_batch, t_len // tm),
        in_specs=_halo_specs(tm, t_len) + [
            _full((1, D_MODEL)), _mod_spec(3, is_ctx, nb), _mod_spec(4, is_ctx, nb), _mod_spec(5, is_ctx, nb),
            _full(w["wup"].shape), _full(w["conv"].shape), _full(w["wdn"].shape), _full((1, D_MODEL))],
        out_specs=_tok_spec(tm, D_MODEL),
        scratch_shapes=[pltpu.VMEM((tm + 2 * HALO, D_MODEL), f32),
                        pltpu.VMEM((2, 2, tm + 2 * HALO, FFN_CW), f32),
                        pltpu.VMEM((tm, D_FF), bf16)],
        compiler_params=_params(2),
        name="ffn_ctx" if is_ctx else "ffn_lat",
    )(x, x, x, norm_g, mods, mods, mods, w["wup"], w["conv"], w["wdn"], final_g)


def _mla_proj_kernel(*refs, with_q):
    if with_q:
        (x_ref, g_ref, sh_ref, sc_ref, win_ref, qng_ref, kvng_ref, wkn_ref, wv_ref,
         wqn_ref, wqr_ref, wqrs_ref, cos_ref, sin_ref,
         kn_ref, v_ref, kr_ref, qn_ref, qr_ref) = refs
    else:
        (x_ref, g_ref, sh_ref, sc_ref, win_ref, kvng_ref, wkn_ref, wv_ref,
         kn_ref, v_ref, kr_ref) = refs
    hb = _norm_mod(x_ref[...], g_ref[...], sh_ref[...], sc_ref[...]).astype(bf16)
    z = _dot(hb, win_ref[...])
    off = C_Q_LORA if with_q else 0
    ckvn = _rms(z[:, off:off + C_KV_LORA], kvng_ref[...]).astype(bf16)
    kn_ref[...] = _dot(ckvn, wkn_ref[...]).astype(bf16)
    v_ref[...] = _dot(ckvn, wv_ref[...]).astype(bf16)
    kr = z[:, off + C_KV_LORA:off + C_KV_LORA + C_ROPE]
    if not with_q:
        kr_ref[...] = kr.astype(bf16)
        return
    krs = z[:, off + C_KV_LORA + C_ROPE:off + C_KV_LORA + 2 * C_ROPE]
    cos2, sin2 = cos_ref[...], sin_ref[...]
    kr_ref[...] = (kr * cos2[:, 0:C_ROPE] + krs * sin2[:, 0:C_ROPE]).astype(bf16)
    cqn = _rms(z[:, 0:C_Q_LORA], qng_ref[...]).astype(bf16)
    qn_ref[...] = _dot(cqn, wqn_ref[...]).astype(bf16)
    qr = _dot(cqn, wqr_ref[...])
    qrs = _dot(cqn, wqrs_ref[...])
    for hp in range(C_HEADS // 2):
        cols = slice(hp * 2 * C_ROPE, (hp + 1) * 2 * C_ROPE)
        rot = (qr[:, cols] * cos2 + qrs[:, cols] * sin2).astype(bf16)
        qr_ref[2 * hp] = rot[:, 0:C_ROPE]
        qr_ref[2 * hp + 1] = rot[:, C_ROPE:]


def _mla_proj(x, mods, norm_g, w, with_q, tm):
    n_batch, t_len, _ = x.shape
    nb = mods.shape[0] - 1
    is_ctx = not with_q
    hw = C_HEADS * C_NOPE
    tok = lambda width: jax.ShapeDtypeStruct((n_batch, t_len, width), bf16)
    out_shape = [tok(hw), tok(C_HEADS * C_VDIM), tok(C_ROPE)]
    out_specs = [_tok_spec(tm, hw), _tok_spec(tm, C_HEADS * C_VDIM), _tok_spec(tm, C_ROPE)]
    in_specs = [_tok_spec(tm, D_MODEL), _full((1, D_MODEL)), _mod_spec(0, is_ctx, nb), _mod_spec(1, is_ctx, nb)]
    if with_q:
        args = [w["win"], w["qng"], w["kvng"], w["wkn"], w["wv"], w["wqn"], w["wqr"], w["wqrs"]]
        in_specs += [_full(a.shape) for a in args]
        in_specs += [pl.BlockSpec((tm, 2 * C_ROPE), lambda b, i: (i, 0))] * 2
        args += [w["cos2"], w["sin2"]]
        out_shape += [tok(hw), jax.ShapeDtypeStruct((n_batch, C_HEADS, t_len, C_ROPE), bf16)]
        out_specs += [_tok_spec(tm, hw), pl.BlockSpec((None, C_HEADS, tm, C_ROPE), lambda b, i: (b, 0, i, 0))]
    else:
        args = [w["win_kv"], w["kvng"], w["wkn"], w["wv"]]
        in_specs += [_full(a.shape) for a in args]
    return pl.pallas_call(
        functools.partial(_mla_proj_kernel, with_q=with_q),
        out_shape=tuple(out_shape),
        grid=(n_batch, t_len // tm),
        in_specs=in_specs,
        out_specs=tuple(out_specs),
        compiler_params=_params(2),
        name="mla_proj_lat" if with_q else "mla_proj_ctx",
    )(x, norm_g, mods, mods, *args)


def _attn_kernel(qn_ref, qr_ref, knc_ref, krc_ref, vc_ref, knl_ref, krl_ref, vl_ref, o_ref,
                 kc_s, kl_s, q_s, *, tq, heads):
    pad = C_NOPE - C_ROPE
    for hh in range(heads):
        cols = slice(hh * C_NOPE, (hh + 1) * C_NOPE)
        for dst, nope, rope in ((kc_s, knc_ref[:, cols], krc_ref[...]), (kl_s, knl_ref[:, cols], krl_ref[...]),
                                (q_s, qn_ref[:, cols], qr_ref[hh])):
            dst[hh, :, 0:C_NOPE] = nope
            dst[hh, :, C_NOPE:C_NOPE + C_ROPE] = rope
            dst[hh, :, C_NOPE + C_ROPE:] = jnp.zeros((dst.shape[1], pad), bf16)
    n_tiles = q_s.shape[1] // tq

    def scores(hh, i):
        q = q_s[hh, pl.ds(i * tq, tq), :]
        return (lax.dot_general(q, kc_s[hh], NT, preferred_element_type=f32),
                lax.dot_general(q, kl_s[hh], NT, preferred_element_type=f32))

    def finish(hh, i, sc, sl):
        cols = slice(hh * C_VDIM, (hh + 1) * C_VDIM)
        m = jnp.maximum(jnp.max(sc, axis=-1, keepdims=True), jnp.max(sl, axis=-1, keepdims=True))
        pc = jnp.exp2(sc - m)
        pl_ = jnp.exp2(sl - m)
        denom = jnp.sum(pc, axis=-1, keepdims=True) + jnp.sum(pl_, axis=-1, keepdims=True)
        o = _dot(pc.astype(bf16), vc_ref[:, cols]) + _dot(pl_.astype(bf16), vl_ref[:, cols])
        o_ref[pl.ds(i * tq, tq), cols] = (o / denom).astype(bf16)

    order = [(hh, i) for hh in range(heads) for i in range(n_tiles)]
    nxt = scores(*order[0])
    for n, tile in enumerate(order):
        cur = nxt
        if n + 1 < len(order):
            nxt = scores(*order[n + 1])
        finish(*tile, *cur)


def _attention(qn, qr, kn_c, kr_c, v_c, kn_l, kr_l, v_l, tq, heads):
    n_batch, t_lat, _ = qn.shape
    t_ctx = kn_c.shape[1]
    group = lambda t: pl.BlockSpec((None, t, heads * C_NOPE), lambda b, h: (b, 0, h))
    rope = lambda t: pl.BlockSpec((None, t, C_ROPE), lambda b, h: (b, 0, 0))
    return pl.pallas_call(
        functools.partial(_attn_kernel, tq=tq, heads=heads),
        out_shape=jax.ShapeDtypeStruct((n_batch, t_lat, C_HEADS * C_VDIM), bf16),
        grid=(n_batch, C_HEADS // heads),
        in_specs=[group(t_lat), pl.BlockSpec((None, heads, t_lat, C_ROPE), lambda b, h: (b, h, 0, 0)),
                  group(t_ctx), rope(t_ctx), group(t_ctx), group(t_lat), rope(t_lat), group(t_lat)],
        out_specs=group(t_lat),
        scratch_shapes=[pltpu.VMEM((heads, t_ctx, 2 * C_NOPE), bf16), pltpu.VMEM((heads, t_lat, 2 * C_NOPE), bf16),
                        pltpu.VMEM((heads, t_lat, 2 * C_NOPE), bf16)],
        compiler_params=_params(2),
        name="mla_attention",
    )(qn, qr, kn_c, kr_c, v_c, kn_l, kr_l, v_l)


def _proj_res_kernel(x_ref, y_ref, w_ref, g1_ref, o_ref):
    o_ref[...] = x_ref[...] + g1_ref[...] * _dot(y_ref[...], w_ref[...])


def _proj_res(x, y, w_out, mods, tm):
    n_batch, t_len, _ = x.shape
    nb = mods.shape[0] - 1
    return pl.pallas_call(
        _proj_res_kernel,
        out_shape=jax.ShapeDtypeStruct(x.shape, f32),
        grid=(n_batch, t_len // tm),
        in_specs=[_tok_spec(tm, D_MODEL), _tok_spec(tm, y.shape[-1]), _full(w_out.shape),
                  _mod_spec(2, False, nb)],
        out_specs=_tok_spec(tm, D_MODEL),
        compiler_params=_params(2),
        name="mla_out_proj",
    )(x, y, w_out, mods)


def _rope_tables(n):
    rows = n // GRID_W
    row = jnp.repeat(jnp.arange(rows, dtype=f32), GRID_W)
    col = jnp.tile(jnp.arange(GRID_W, dtype=f32), rows)
    n_freq = C_ROPE // 4
    inv = ROPE_THETA ** (-jnp.arange(n_freq, dtype=f32) / n_freq)
    ang = jnp.concatenate([row[:, None] * inv, col[:, None] * inv], axis=-1)
    cos, sin = jnp.cos(ang), jnp.sin(ang)
    cos64 = jnp.concatenate([cos, cos], axis=-1)
    sin64 = jnp.concatenate([-sin, sin], axis=-1)
    return jnp.tile(cos64, (1, 2)), jnp.tile(sin64, (1, 2))


def _swap_halves(w):
    h = C_ROPE // 2
    return jnp.concatenate([w[..., h:], w[..., :h]], axis=-1)


def _ffn_weights(w_up, conv_w, w_down):
    return {"wup": w_up.astype(bf16), "conv": conv_w, "wdn": w_down.astype(bf16)}


def kernel(x, c, ctx, c_ctx, ada_w, ada_b, norm1_g, norm2_g, ab_w_in, a_ln_g, a_ln_b, a_ws, a_bs, b_conv_w, b_a_log, b_dt_bias, b_norm_g, ab_w_out, mla_w_in, mla_q_norm_g, mla_kv_norm_g, mla_w_uq, mla_w_ukv, mla_w_out, ffn_w_up, ffn_conv_w, ffn_w_down, final_g):
    n_batch, t_lat, d = x.shape
    t_ctx = ctx.shape[1]
    tm_lat, tm_ctx = 512, t_ctx
    tm_wide = 2 * tm_lat

    cc = jnp.concatenate([c, c_ctx[None, :], jnp.zeros((7, d), f32)], axis=0)
    mods_all = _ada_mods(cc, ada_w, ada_b)
    mods = [mods_all[i, :n_batch + 1].reshape(n_batch + 1, 6, 1, d) for i in range(2)]
    row = lambda v: v.reshape(1, -1)

    a2, b4 = 2 * A_WIDTH, 4 * B_WIDTH
    w_in = ab_w_in[0]
    w_ab = w_in[:, a2 + b4:]
    w0 = {
        "wqkv": w_in[:, a2:a2 + 3 * B_WIDTH].astype(bf16),
        "wrest": jnp.concatenate([w_in[:, :a2], w_in[:, a2 + 3 * B_WIDTH:a2 + b4]], axis=1).astype(bf16),
        "wab": jnp.pad(w_ab, ((0, 0), (0, 128 - 4 * B_HEADS))).astype(bf16),
        "wabt": w_ab.T.astype(bf16),
        "conv": b_conv_w[0],
    }
    parts_c = _inproj0(ctx, mods[0], row(norm1_g[0]), w0, True, tm_ctx)
    parts_l = _inproj0(x, mods[0], row(norm1_g[0]), w0, False, tm_wide)
    yb_c, yb_l = _gdn(parts_c[1:], parts_l[1:], b_a_log[0], b_dt_bias[0], row(b_norm_g[0]))
    wm = {"lng": row(a_ln_g[0]), "lnb": row(a_ln_b[0]), "ws": a_ws[0].astype(bf16), "bst": a_bs[0].T,
          "wout": ab_w_out[0].astype(bf16)}
    flat = lambda a: a.reshape(1, n_batch * t_ctx, a.shape[-1])
    unflat = lambda a: a.reshape(n_batch, t_ctx, a.shape[-1])
    tm_flat = math.gcd(tm_wide, n_batch * t_ctx)
    cx = unflat(_mix0(flat(ctx), flat(parts_c[0]), flat(yb_c), mods[0], wm, True, tm_flat))
    lat = _mix0(x, parts_l[0], yb_l, mods[0], wm, False, tm_wide)
    wf = _ffn_weights(ffn_w_up[0], ffn_conv_w[0], ffn_w_down[0])
    cx = _ffn(cx, mods[0], row(norm2_g[0]), wf, row(final_g), True, tm_ctx, False)
    lat = _ffn(lat, mods[0], row(norm2_g[0]), wf, row(final_g), False, tm_lat, False)

    w_in = mla_w_in[0]
    kv0 = C_Q_LORA
    kr0 = C_Q_LORA + C_KV_LORA
    q_fold = (C_NOPE + C_ROPE) ** -0.5 * math.log2(math.e)
    w_uq = (mla_w_uq[0] * q_fold).reshape(C_Q_LORA, C_HEADS, C_NOPE + C_ROPE)
    w_ukv = mla_w_ukv[0].reshape(C_KV_LORA, C_HEADS, C_NOPE + C_VDIM)
    w_qr = w_uq[:, :, C_NOPE:]
    cos2, sin2 = _rope_tables(t_lat)
    w1 = {
        "win": jnp.concatenate([w_in, _swap_halves(w_in[:, kr0:])], axis=1).astype(bf16),
        "win_kv": jnp.pad(w_in[:, kv0:], ((0, 0), (0, C_ROPE))).astype(bf16),
        "qng": row(mla_q_norm_g[0]), "kvng": row(mla_kv_norm_g[0]),
        "wkn": w_ukv[:, :, :C_NOPE].reshape(C_KV_LORA, -1).astype(bf16),
        "wv": w_ukv[:, :, C_NOPE:].reshape(C_KV_LORA, -1).astype(bf16),
        "wqn": w_uq[:, :, :C_NOPE].reshape(C_Q_LORA, -1).astype(bf16),
        "wqr": w_qr.reshape(C_Q_LORA, -1).astype(bf16),
        "wqrs": _swap_halves(w_qr).reshape(C_Q_LORA, -1).astype(bf16),
        "cos2": cos2, "sin2": sin2,
    }
    kn_c, v_c, kr_c = (unflat(a) for a in _mla_proj(flat(cx), mods[1], row(norm1_g[1]), w1, False, tm_flat))
    kn_l, v_l, kr_l, qn, qr = _mla_proj(lat, mods[1], row(norm1_g[1]), w1, True, tm_wide)
    att = _attention(qn, qr, kn_c, kr_c, v_c, kn_l, kr_l, v_l, 256, 2)
    lat = _proj_res(lat, att, mla_w_out[0].astype(bf16), mods[1], tm_wide)
    wf = _ffn_weights(ffn_w_up[1], ffn_conv_w[1], ffn_w_down[1])
    return _ffn(lat, mods[1], row(norm2_g[1]), wf, row(final_g), False, tm_lat, True)
```
